```python
import jax, jax.numpy as jnp
from jax import lax
import numpy as np

D_MODEL = 1024
BATCH = 4
SEQ = 4096
DEPTH = 4

HG_HEADS = 4
HG_DK = 64
HG_DV = 64
HG_CHUNK = 64
MLA_HEADS = 4
MLA_Q_RANK = 256
MLA_KV_RANK = 128
MLA_NOPE = 64
MLA_ROPE = 32
MLA_V = 64
MLA_QK = MLA_NOPE + MLA_ROPE
ROPE_BASE = 10000.0
CONV_CH = 256
CONV_K = 31
SB_HEADS = 4
SB_HEAD_DIM = 64
N_BRANCH = 4
BRANCH_W = 256
D_FF = 2816
FFN_CONV_K = 3
ATTN_BLOCK = 128
EPS = 1e-6

IN_COLS = (HG_HEADS * HG_DK, HG_HEADS * HG_DK, HG_HEADS * HG_DV, HG_HEADS * HG_DV,
           MLA_Q_RANK, MLA_KV_RANK, MLA_ROPE,
           2 * CONV_CH,
           SB_HEADS * SB_HEAD_DIM, SB_HEADS * SB_HEAD_DIM, SB_HEADS * SB_HEAD_DIM,
           N_BRANCH * D_MODEL)
D_IN = 4 * 256 + 256 + 128 + 32 + 512 + 3 * 256 + 4 * 1024

kernel_name = "hybrid_gated_parallel_mixer_block"

F32 = jnp.float32


def _rmsnorm(x, g):
    xf = x.astype(F32)
    y = xf * lax.rsqrt(jnp.mean(xf * xf, axis=-1, keepdims=True) + EPS)
    return (y * g.astype(F32)).astype(x.dtype)


def _layernorm(x, g, b):
    xf = x.astype(F32)
    xc = xf - jnp.mean(xf, axis=-1, keepdims=True)
    var = jnp.mean(xc * xc, axis=-1, keepdims=True)
    return (xc * lax.rsqrt(var + 1e-5) * g.astype(F32) + b.astype(F32)).astype(x.dtype)


def _causal_dwconv(u, w, b):
    k, c = w.shape
    out = lax.conv_general_dilated(
        u, w[:, None, :].astype(u.dtype), window_strides=(1,), padding=[(k - 1, 0)],
        dimension_numbers=('NWC', 'WIO', 'NWC'), feature_group_count=c)
    return out + b.astype(u.dtype)


def _rotary(x, pos):
    half = x.shape[-1] // 2
    freqs = ROPE_BASE ** (-jnp.arange(half, dtype=F32) / half)
    ang = pos.astype(F32)[:, None] * freqs[None, :]
    cos = jnp.cos(ang)[None, :, None, :]
    sin = jnp.sin(ang)[None, :, None, :]
    xf = x.astype(F32)
    x1, x2 = xf[..., :half], xf[..., half:]
    return jnp.concatenate([x1 * cos - x2 * sin, x2 * cos + x1 * sin], axis=-1).astype(x.dtype)


def _to_blocks(t, size):
    b, h, s, d = t.shape
    return t.reshape(b, h, s // size, size, d).transpose(2, 0, 1, 3, 4)


def _from_blocks(t):
    nb, b, h, size, d = t.shape
    return t.transpose(1, 2, 0, 3, 4).reshape(b, h, nb * size, d)


def _chunk_gated_recurrence(q, k, v, log_f):
    b_, h_, _, dk = q.shape
    dv = v.shape[-1]
    causal = jnp.tril(jnp.ones((HG_CHUNK, HG_CHUNK), dtype=bool))

    def step(state, inp):
        qc, kc, vc, lfc = inp
        bcum = jnp.cumsum(lfc, axis=2)
        o_inter = jnp.einsum('bhtk,bhkv->bhtv', qc * jnp.exp(bcum), state)
        diff = jnp.where(causal[:, :, None],
                         bcum[:, :, :, None, :] - bcum[:, :, None, :, :], -jnp.inf)
        a = jnp.einsum('bhtk,bhsk,bhtsk->bhts', qc, kc, jnp.exp(diff))
        o_intra = jnp.einsum('bhts,bhsv->bhtv', a, vc)
        b_last = bcum[:, :, -1:, :]
        new_state = (jnp.exp(b_last[:, :, 0, :])[..., None] * state
                     + jnp.einsum('bhsk,bhsv->bhkv', kc * jnp.exp(b_last - bcum), vc))
        return new_state, o_inter + o_intra

    s0 = jnp.zeros((b_, h_, dk, dv), F32)
    xs = (_to_blocks(q, HG_CHUNK), _to_blocks(k, HG_CHUNK),
          _to_blocks(v, HG_CHUNK), _to_blocks(log_f, HG_CHUNK))
    _, o = lax.scan(step, s0, xs)
    return _from_blocks(o)


def _causal_softmax_attn(q, k, v, scale):
    s = q.shape[2]
    kpos = jnp.arange(s)

    def one(args):
        qi, i = args
        sc = jnp.einsum('bhtd,bhsd->bhts', qi, k).astype(F32) * scale
        qpos = i * ATTN_BLOCK + jnp.arange(ATTN_BLOCK)
        sc = jnp.where(kpos[None, :] <= qpos[:, None], sc, -jnp.inf)
        p = jax.nn.softmax(sc, axis=-1)
        return jnp.einsum('bhts,bhsv->bhtv', p.astype(v.dtype), v)

    out = lax.map(one, (_to_blocks(q, ATTN_BLOCK), jnp.arange(s // ATTN_BLOCK)))
    return _from_blocks(out)


def _stick_breaking_attn(q, k, v, scale):
    s = q.shape[2]
    kpos = jnp.arange(s)

    def one(args):
        qi, i = args
        z = jnp.einsum('bhtd,bhsd->bhts', qi, k).astype(F32) * scale
        qpos = i * ATTN_BLOCK + jnp.arange(ATTN_BLOCK)
        mask = kpos[None, :] < qpos[:, None]
        log_beta = jax.nn.log_sigmoid(z)
        log_1mb = jnp.where(mask, jax.nn.log_sigmoid(-z), 0.0)
        after = lax.cumsum(log_1mb, axis=3, reverse=True) - log_1mb
        w = jnp.where(mask, jnp.exp(log_beta + after), 0.0)
        return jnp.einsum('bhts,bhsv->bhtv', w.astype(v.dtype), v)

    out = lax.map(one, (_to_blocks(q, ATTN_BLOCK), jnp.arange(s // ATTN_BLOCK)))
    return _from_blocks(out)


def _heads(t, n, d):
    b, s, _ = t.shape
    return t.reshape(b, s, n, d).transpose(0, 2, 1, 3)


def _hgrn2_branch(hq, hf, hv, hg, lb, g_out):
    b, s, _ = hq.shape
    lb = lb.astype(F32)
    ff = hf.astype(F32)
    q = jax.nn.silu(hq.astype(F32))
    log_f = jnp.logaddexp(jnp.log(lb), jnp.log1p(-lb) + jax.nn.log_sigmoid(ff))
    k = (1.0 - lb) * jax.nn.sigmoid(-ff)
    o = _chunk_gated_recurrence(_heads(q, HG_HEADS, HG_DK), _heads(k, HG_HEADS, HG_DK),
                                _heads(hv.astype(F32), HG_HEADS, HG_DV),
                                _heads(log_f, HG_HEADS, HG_DK))
    o = _rmsnorm(o.transpose(0, 2, 1, 3), g_out)
    o = o * jax.nn.silu(hg.astype(F32)).reshape(b, s, HG_HEADS, HG_DV)
    return o.reshape(b, s, HG_HEADS * HG_DV).astype(hq.dtype)


def _mla_branch(cq, ckv, kr, g_q_lat, w_uq, g_kv_lat, w_ukv, g_qk_q, g_qk_k, pos):
    b, s, _ = cq.shape
    q = (_rmsnorm(cq, g_q_lat) @ w_uq).reshape(b, s, MLA_HEADS, MLA_QK)
    kv = (_rmsnorm(ckv, g_kv_lat) @ w_ukv).reshape(b, s, MLA_HEADS, MLA_NOPE + MLA_V)
    k_nope, v = kv[..., :MLA_NOPE], kv[..., MLA_NOPE:]
    k_rope = jnp.broadcast_to(kr[:, :, None, :], (b, s, MLA_HEADS, MLA_ROPE))
    k = jnp.concatenate([k_nope, k_rope], axis=-1)
    q = _rmsnorm(q, g_qk_q)
    k = _rmsnorm(k, g_qk_k)
    q = jnp.concatenate([q[..., :MLA_NOPE], _rotary(q[..., MLA_NOPE:], pos)], axis=-1)
    k = jnp.concatenate([k[..., :MLA_NOPE], _rotary(k[..., MLA_NOPE:], pos)], axis=-1)
    o = _causal_softmax_attn(q.transpose(0, 2, 1, 3), k.transpose(0, 2, 1, 3),
                             v.transpose(0, 2, 1, 3), MLA_QK ** -0.5)
    return o.transpose(0, 2, 1, 3).reshape(b, s, MLA_HEADS * MLA_V)


def _conformer_conv_branch(cu, conv_w, conv_b, ln_g, ln_b):
    a, g = jnp.split(cu, 2, axis=-1)
    u = a * jax.nn.sigmoid(g)
    u = _causal_dwconv(u, conv_w, conv_b)
    u = _layernorm(u, ln_g, ln_b)
    return jax.nn.silu(u)


def _stick_breaking_branch(sq, sk, sv):
    b, s, _ = sq.shape
    o = _stick_breaking_attn(_heads(sq, SB_HEADS, SB_HEAD_DIM), _heads(sk, SB_HEADS, SB_HEAD_DIM),
                             _heads(sv, SB_HEADS, SB_HEAD_DIM), SB_HEAD_DIM ** -0.5)
    return o.transpose(0, 2, 1, 3).reshape(b, s, SB_HEADS * SB_HEAD_DIM)


def setup_inputs(seed: int = 0) -> dict:
    key = jax.random.key(seed)
    ks = jax.random.split(key, 24)

    def nrm(k, shape, scale):
        return jax.random.normal(k, shape, F32) * scale

    def gain(k, shape):
        return 1.0 + 0.1 * jax.random.normal(k, shape, F32)

    res_scale = (2.0 * DEPTH) ** -0.5
    L = DEPTH
    return {
        "x": nrm(ks[0], (BATCH, SEQ, D_MODEL), 1.0),
        "g_mix": gain(ks[1], (L, D_MODEL)),
        "w_in": nrm(ks[2], (L, D_MODEL, D_IN), D_MODEL ** -0.5),
        "lb_logits": nrm(ks[3], (L, HG_HEADS * HG_DK), 0.5),
        "g_hg_out": gain(ks[4], (L, HG_DV)),
        "g_q_lat": gain(ks[5], (L, MLA_Q_RANK)),
        "w_uq": nrm(ks[6], (L, MLA_Q_RANK, MLA_HEADS * MLA_QK), MLA_Q_RANK ** -0.5),
        "g_kv_lat": gain(ks[7], (L, MLA_KV_RANK)),
        "w_ukv": nrm(ks[8], (L, MLA_KV_RANK, MLA_HEADS * (MLA_NOPE + MLA_V)), MLA_KV_RANK ** -0.5),
        "g_qk_q": gain(ks[9], (L, MLA_QK)),
        "g_qk_k": gain(ks[10], (L, MLA_QK)),
        "conv_w": nrm(ks[11], (L, CONV_K, CONV_CH), CONV_K ** -0.5),
        "conv_b": nrm(ks[12], (L, CONV_CH), 0.02),
        "conv_ln_g": gain(ks[13], (L, CONV_CH)),
        "conv_ln_b": nrm(ks[14], (L, CONV_CH), 0.02),
        "w_branch": nrm(ks[15], (L, N_BRANCH, BRANCH_W, D_MODEL), BRANCH_W ** -0.5),
        "w_out": nrm(ks[16], (L, D_MODEL, D_MODEL), D_MODEL ** -0.5 * res_scale),
        "g_ffn": gain(ks[17], (L, D_MODEL)),
        "w_up": nrm(ks[18], (L, D_MODEL, 2 * D_FF), D_MODEL ** -0.5),
        "ffn_conv_w": nrm(ks[19], (L, FFN_CONV_K, 2 * D_FF), FFN_CONV_K ** -0.5),
        "ffn_conv_b": nrm(ks[20], (L, 2 * D_FF), 0.02),
        "w_down": nrm(ks[21], (L, D_FF, D_MODEL), D_FF ** -0.5 * res_scale),
    }


def reference(x, g_mix, w_in, lb_logits, g_hg_out, g_q_lat, w_uq, g_kv_lat, w_ukv,
              g_qk_q, g_qk_k, conv_w, conv_b, conv_ln_g, conv_ln_b, w_branch, w_out,
              g_ffn, w_up, ffn_conv_w, ffn_conv_b, w_down):
    b, s, _ = x.shape
    pos = jnp.arange(s)
    lb_all = jnp.cumsum(jax.nn.softmax(lb_logits.astype(F32), axis=0), axis=0)
    lb_all = lb_all - lb_all[0:1]
    offs = np.cumsum(np.array(IN_COLS))[:-1].tolist()
    for l in range(DEPTH):
        h = _rmsnorm(x, g_mix[l])
        proj = h @ w_in[l]
        (hq, hf, hv, hg, cq, ckv, kr, cu, sq, sk, sv, gl) = jnp.split(proj, offs, axis=-1)
        y_a = _hgrn2_branch(hq, hf, hv, hg, lb_all[l], g_hg_out[l])
        y_b = _mla_branch(cq, ckv, kr, g_q_lat[l], w_uq[l], g_kv_lat[l], w_ukv[l],
                          g_qk_q[l], g_qk_k[l], pos)
        y_c = _conformer_conv_branch(cu, conv_w[l], conv_b[l], conv_ln_g[l], conv_ln_b[l])
        y_d = _stick_breaking_branch(sq, sk, sv)
        ys = jnp.stack([y_a, y_b, y_c, y_d], axis=2)
        br = jnp.einsum('bsnc,ncd->bsnd', ys, w_branch[l])
        gates = jax.nn.sigmoid(gl.reshape(b, s, N_BRANCH, D_MODEL))
        merged = jnp.einsum('bsnd,bsnd->bsd', gates, br)
        x = x + merged @ w_out[l]
        h2 = _rmsnorm(x, g_ffn[l])
        u = _causal_dwconv(h2 @ w_up[l], ffn_conv_w[l], ffn_conv_b[l])
        val, gate = jnp.split(u, 2, axis=-1)
        x = x + (jax.nn.silu(gate) * val) @ w_down[l]
    return x
```

```python
import functools

import jax
import jax.numpy as jnp
import numpy as np
from jax import lax
from jax.experimental import pallas as pl
from jax.experimental.pallas import tpu as pltpu

F32 = jnp.float32
BF16 = jnp.bfloat16

D_MODEL = 1024
HG_HEADS = 4
HG_DK = 64
HG_DV = 64
MLA_HEADS = 4
MLA_Q_RANK = 256
MLA_KV_RANK = 128
MLA_NOPE = 64
MLA_ROPE = 32
MLA_V = 64
MLA_QK = MLA_NOPE + MLA_ROPE
ROPE_BASE = 10000.0
CONV_CH = 256
CONV_K = 31
SB_HEADS = 4
SB_HEAD_DIM = 64
N_BRANCH = 4
BRANCH_W = 256
D_FF = 2816
FFN_CONV_K = 3
EPS = 1e-6
LN_EPS = 1e-5

IN_COLS = (256, 256, 256, 256, MLA_Q_RANK, MLA_KV_RANK, MLA_ROPE, 2 * CONV_CH, 256, 256, 256,
           N_BRANCH * D_MODEL)

LANES = 128
BF16_ROWS = 16
VMEM_LIMIT = 56 * 1024 * 1024

COL_GATE = 0
COL_HG = 4096
COL_MLA = 5120
COL_CU = 5632
COL_SB = 6144
N_IN = 6912
MLA_W = 512
HEAD_PAD = 128

HG_CHUNK = 64
HG_SUB = 16
HG_W = HG_HEADS * HG_DK


def _cparams(sem):
    return pltpu.CompilerParams(dimension_semantics=sem, vmem_limit_bytes=VMEM_LIMIT)


def _sigmoid(x):
    return 1.0 / (1.0 + jnp.exp(-x))


def _split_bf16(x):
    hi = x.astype(BF16)
    lo = (x - hi.astype(F32)).astype(BF16)
    return hi, lo


def _dot(a, b):
    return jnp.dot(a, b, preferred_element_type=F32)


def _dot_nt(a, b):
    return lax.dot_general(a, b, (((1,), (1,)), ((), ())), preferred_element_type=F32)


def _dot_tn(a, b):
    return lax.dot_general(a, b, (((0,), (0,)), ((), ())), preferred_element_type=F32)


def _norm_matmul_kernel(x_ref, g_ref, w_ref, o_ref, h_ref):
    @pl.when(pl.program_id(1) == 0)
    def _():
        x = x_ref[...]
        ms = jnp.mean(x * x, axis=-1, keepdims=True)
        h_ref[...] = (x * lax.rsqrt(ms + EPS) * g_ref[...]).astype(BF16)

    o_ref[...] = _dot(h_ref[...], w_ref[...]).astype(o_ref.dtype)


def _norm_matmul(x2d, g, w, tm, tn):
    t, d = x2d.shape
    n = w.shape[1]
    return pl.pallas_call(
        _norm_matmul_kernel,
        grid=(t // tm, n // tn),
        in_specs=[pl.BlockSpec((tm, d), lambda i, j: (i, 0)),
                  pl.BlockSpec((1, d), lambda i, j: (0, 0)),
                  pl.BlockSpec((d, tn), lambda i, j: (0, j))],
        out_specs=pl.BlockSpec((tm, tn), lambda i, j: (i, j)),
        out_shape=jax.ShapeDtypeStruct((t, n), BF16),
        scratch_shapes=[pltpu.VMEM((tm, d), BF16)],
        compiler_params=_cparams(("parallel", "arbitrary")),
        name="norm_matmul",
    )(x2d, g, w)


def _hgrn_kernel(p_ref, lbl_ref, lmask_ref, gout_ref, o_ref, st_ref, *, n_chunks):
    @pl.when(pl.program_id(1) == 0)
    def _():
        st_ref[...] = jnp.zeros_like(st_ref)

    c = HG_CHUNK
    w = HG_W
    logits = lbl_ref[...]
    mx = jnp.max(logits, axis=0, keepdims=True)
    ex = jnp.exp(logits - mx)
    sm = ex / jnp.sum(ex, axis=0, keepdims=True)
    lb = jnp.sum(sm * lmask_ref[...], axis=0, keepdims=True)
    log_lb = jnp.log(lb)
    log_1mlb = jnp.log1p(-lb)
    one_mlb = 1.0 - lb
    gout = gout_ref[...]

    row = lax.broadcasted_iota(jnp.int32, (c, c), 0)
    col = lax.broadcasted_iota(jnp.int32, (c, c), 1)
    tri = (col <= row).astype(BF16)
    rmod = lax.broadcasted_iota(jnp.int32, (c, w), 0) % HG_SUB
    lane_head = lax.broadcasted_iota(jnp.int32, (c, w), 1) // HG_DK
    eh_r = lax.broadcasted_iota(jnp.int32, (w, w), 0) // HG_DK
    eh_c = lax.broadcasted_iota(jnp.int32, (w, w), 1) // HG_DK
    same_head = eh_r == eh_c
    eh = same_head.astype(BF16)
    n_sub = c // HG_SUB
    cat_head = lax.broadcasted_iota(jnp.int32, (c, (n_sub - 1) * w), 1) % w // HG_DK

    def chunk(ci, carry):
        r0 = pl.multiple_of(ci * c, c)
        blk = p_ref[pl.ds(r0, c), :].astype(F32)
        hq, hf, hv, hg = blk[:, :w], blk[:, w:2 * w], blk[:, 2 * w:3 * w], blk[:, 3 * w:]
        q = hq * _sigmoid(hq)
        e = jnp.exp(-jnp.abs(hf))
        l1pe = jnp.log(1.0 + e)
        logsig = jnp.minimum(hf, 0.0) - l1pe
        b_term = log_1mlb + logsig
        mxab = jnp.maximum(log_lb, b_term)
        lf = mxab + jnp.log(1.0 + jnp.exp(-jnp.abs(log_lb - b_term)))
        kk = one_mlb * jnp.where(hf >= 0.0, e, 1.0) / (1.0 + e)
        v_bf = hv.astype(BF16)

        lf_hi, lf_lo = _split_bf16(lf)
        bcum = _dot(tri, lf_hi) + _dot(tri, lf_lo)
        b_last = bcum[c - 1:c, :]

        st = st_ref[...]
        qe = (q * jnp.exp(bcum)).astype(BF16)
        o = _dot_nt(qe, st.astype(BF16))

        qms, kms = [], []
        for i in range(1, n_sub):
            lo_r, hi_r = i * HG_SUB, (i + 1) * HG_SUB
            ref_row = bcum[lo_r - 1:lo_r, :]
            qi = q[lo_r:hi_r] * jnp.exp(bcum[lo_r:hi_r] - ref_row)
            pieces = [jnp.zeros((lo_r, w), F32), qi]
            if hi_r < c:
                pieces.append(jnp.zeros((c - hi_r, w), F32))
            qms.append(jnp.concatenate(pieces, axis=0))
            ki = kk[:lo_r] * jnp.exp(ref_row - bcum[:lo_r])
            kms.append(jnp.concatenate([ki, jnp.zeros((c - lo_r, w), F32)], axis=0))
        qc = jnp.concatenate(qms, axis=1)
        kc = jnp.concatenate(kms, axis=1).astype(BF16)
        qstack = jnp.concatenate(
            [jnp.where(cat_head == h, qc, 0.0) for h in range(HG_HEADS)], axis=0).astype(BF16)
        a_stack = _dot_nt(qstack, kc)
        r_full = _dot(a_stack.astype(BF16), v_bf)
        for h in range(HG_HEADS):
            o = o + jnp.where(lane_head == h, r_full[h * c:(h + 1) * c], 0.0)

        for d in range(HG_SUB):
            if d == 0:
                ks, bs, vs = kk, bcum, hv
            else:
                ks = pltpu.roll(kk, d, 0)
                bs = pltpu.roll(bcum, d, 0)
                vs = pltpu.roll(hv, d, 0)
            valid = rmod >= d
            pd = jnp.where(valid, q * ks * jnp.exp(jnp.where(valid, bcum - bs, 0.0)), 0.0)
            o = o + _dot(pd.astype(BF16), eh) * vs

        kdec = (kk * jnp.exp(b_last - bcum)).astype(BF16)
        upd = _dot_tn(v_bf, kdec)
        st_ref[...] = st * jnp.exp(b_last) + jnp.where(same_head, upd, 0.0)

        o2_hi, o2_lo = _split_bf16(o * o)
        ms = (_dot(o2_hi, eh) + _dot(o2_lo, eh)) * (1.0 / HG_DV)
        y = o * lax.rsqrt(ms + EPS) * gout * (hg * _sigmoid(hg))
        o_ref[pl.ds(r0, c), :] = y.astype(o_ref.dtype)
        return carry

    lax.fori_loop(0, n_chunks, chunk, 0)


def _hgrn(proj, lb_logits, lmask, gout_t, batch, seq, ts):
    t = proj.shape[0]
    nsb = seq // ts
    cb = COL_HG // (4 * HG_W)
    return pl.pallas_call(
        functools.partial(_hgrn_kernel, n_chunks=ts // HG_CHUNK),
        grid=(batch, nsb),
        in_specs=[pl.BlockSpec((ts, 4 * HG_W), lambda b, s: (b * nsb + s, cb)),
                  pl.BlockSpec(lb_logits.shape, lambda b, s: (0, 0)),
                  pl.BlockSpec(lmask.shape, lambda b, s: (0, 0)),
                  pl.BlockSpec((1, HG_W), lambda b, s: (0, 0))],
        out_specs=pl.BlockSpec((ts, HG_W), lambda b, s: (b * nsb + s, 0)),
        out_shape=jax.ShapeDtypeStruct((t, HG_W), BF16),
        scratch_shapes=[pltpu.VMEM((HG_W, HG_W), F32)],
        compiler_params=_cparams(("parallel", "arbitrary")),
        name="hgrn2",
    )(proj, lb_logits, lmask, gout_t)


def _mla_prep_kernel(p_ref, gq_ref, wuq_ref, gkv_ref, wkc_ref, wuv_ref, gqq_ref, gqk_ref,
                     cos_ref, s1_ref, s2_ref, q_ref, k_ref, v_ref):
    blk = p_ref[...].astype(F32)
    cq = blk[:, :MLA_Q_RANK]
    ckv = blk[:, MLA_Q_RANK:MLA_Q_RANK + MLA_KV_RANK]
    krp = blk[:, MLA_Q_RANK + MLA_KV_RANK:]

    def rms(x, g):
        ms = jnp.mean(x * x, axis=-1, keepdims=True)
        return x * lax.rsqrt(ms + EPS) * g

    cqn = rms(cq, gq_ref[...]).astype(BF16)
    ckvn = rms(ckv, gkv_ref[...]).astype(BF16)
    q_raw = _dot(cqn, wuq_ref[...])
    k_raw = _dot(jnp.concatenate([ckvn, krp.astype(BF16)], axis=1), wkc_ref[...])
    v_ref[...] = _dot(ckvn, wuv_ref[...]).astype(v_ref.dtype)

    cos, s1, s2 = cos_ref[...], s1_ref[...], s2_ref[...]

    def head(x, g, scale):
        ms = jnp.sum(x * x, axis=-1, keepdims=True) * (1.0 / MLA_QK)
        xn = x * lax.rsqrt(ms + EPS) * g
        half = MLA_ROPE // 2
        y = xn * cos + pltpu.roll(xn, half, 1) * s1 + pltpu.roll(xn, HEAD_PAD - half, 1) * s2
        return y * scale

    for h in range(MLA_HEADS):
        sl = slice(h * HEAD_PAD, (h + 1) * HEAD_PAD)
        q_ref[:, sl] = head(q_raw[:, sl], gqq_ref[...], MLA_QK ** -0.5).astype(q_ref.dtype)
        k_ref[:, sl] = head(k_raw[:, sl], gqk_ref[...], 1.0).astype(k_ref.dtype)


def _mla_prep(proj, gq, wuq, gkv, wkc, wuv, gqq, gqk, cos_t, s1_t, s2_t, seq, tm):
    t = proj.shape[0]
    nsb = seq // tm
    cb = COL_MLA // MLA_W
    full = lambda a: pl.BlockSpec(a.shape, lambda i: (0,) * a.ndim)
    tab = pl.BlockSpec((tm, HEAD_PAD), lambda i: (i % nsb, 0))
    qk_w = MLA_HEADS * HEAD_PAD
    return pl.pallas_call(
        _mla_prep_kernel,
        grid=(t // tm,),
        in_specs=[pl.BlockSpec((tm, MLA_W), lambda i: (i, cb)),
                  full(gq), full(wuq), full(gkv), full(wkc), full(wuv), full(gqq), full(gqk),
                  tab, tab, tab],
        out_specs=[pl.BlockSpec((tm, qk_w), lambda i: (i, 0)),
                   pl.BlockSpec((tm, qk_w), lambda i: (i, 0)),
                   pl.BlockSpec((tm, MLA_HEADS * MLA_V), lambda i: (i, 0))],
        out_shape=[jax.ShapeDtypeStruct((t, qk_w), BF16),
                   jax.ShapeDtypeStruct((t, qk_w), BF16),
                   jax.ShapeDtypeStruct((t, MLA_HEADS * MLA_V), BF16)],
        compiler_params=_cparams(("parallel",)),
        name="mla_prep",
    )(proj, gq, wuq, gkv, wkc, wuv, gqq, gqk, cos_t, s1_t, s2_t)


def _mla_attn_kernel(q_ref, k_ref, v_ref, o_ref, *, tq):
    i = pl.program_id(2)
    lane = lax.broadcasted_iota(jnp.int32, (tq, LANES), 1)
    row = lax.broadcasted_iota(jnp.int32, (tq, tq), 0)
    col = lax.broadcasted_iota(jnp.int32, (tq, tq), 1)
    causal = col <= row
    outs = []
    for hh in range(2):
        qh = q_ref[:, hh * HEAD_PAD:(hh + 1) * HEAD_PAD]

        def step(kc, carry, masked):
            m, l, acc = carry
            k0 = pl.multiple_of(kc * tq, tq)
            kb = k_ref[pl.ds(k0, tq), hh * HEAD_PAD:(hh + 1) * HEAD_PAD]
            vb = v_ref[pl.ds(k0, tq), :]
            s = _dot_nt(qh, kb)
            if masked:
                s = jnp.where(causal, s, -jnp.inf)
            m_new = jnp.maximum(m, jnp.max(s, axis=-1, keepdims=True))
            alpha = jnp.exp(m - m_new)
            p = jnp.exp(s - m_new)
            l = alpha * l + jnp.sum(p, axis=-1, keepdims=True)
            acc = alpha * acc + _dot(p.astype(BF16), vb)
            return m_new, l, acc

        init = (jnp.full((tq, 1), -jnp.inf, F32), jnp.zeros((tq, 1), F32),
                jnp.zeros((tq, LANES), F32))
        carry = lax.fori_loop(0, i, lambda kc, cr: step(kc, cr, False), init)
        m, l, acc = step(i, carry, True)
        outs.append(acc / l)
    o_ref[...] = jnp.where(lane < MLA_V, outs[0], outs[1]).astype(o_ref.dtype)


def _mla_attn(q, k, v, batch, seq, tq):
    t = q.shape[0]
    nqb = seq // tq
    return pl.pallas_call(
        functools.partial(_mla_attn_kernel, tq=tq),
        grid=(batch, MLA_HEADS // 2, nqb),
        in_specs=[pl.BlockSpec((tq, 2 * HEAD_PAD), lambda b, p, i: (b * nqb + i, p)),
                  pl.BlockSpec((seq, 2 * HEAD_PAD), lambda b, p, i: (b, p)),
                  pl.BlockSpec((seq, LANES), lambda b, p, i: (b, p))],
        out_specs=pl.BlockSpec((tq, LANES), lambda b, p, i: (b * nqb + i, p)),
        out_shape=jax.ShapeDtypeStruct((t, MLA_HEADS * MLA_V), BF16),
        compiler_params=_cparams(("parallel", "parallel", "arbitrary")),
        name="mla_attn",
    )(q, k, v)


CONV_HALO = 32
CONV_ROWS = 64


def _conv_kernel(p_ref, w_ref, b_ref, g_ref, beta_ref, o_ref, u_ref, *, ts):
    @pl.when(pl.program_id(1) == 0)
    def _():
        u_ref[0:CONV_HALO, :] = jnp.zeros((CONV_HALO, CONV_CH), F32)

    @pl.when(pl.program_id(1) != 0)
    def _():
        u_ref[0:CONV_HALO, :] = u_ref[ts:ts + CONV_HALO, :]

    blk = p_ref[...].astype(F32)
    u_ref[CONV_HALO:, :] = blk[:, :CONV_CH] * _sigmoid(blk[:, CONV_CH:])
    w = w_ref[...]
    bias = b_ref[...]
    gam, beta = g_ref[...], beta_ref[...]
    first = CONV_HALO - (CONV_K - 1)
    for r in range(ts // CONV_ROWS):
        acc = jnp.zeros((CONV_ROWS, CONV_CH), F32) + bias
        for j in range(CONV_K):
            acc = acc + w[j:j + 1, :] * u_ref[pl.ds(r * CONV_ROWS + first + j, CONV_ROWS), :]
        mu = jnp.mean(acc, axis=-1, keepdims=True)
        xc = acc - mu
        var = jnp.mean(xc * xc, axis=-1, keepdims=True)
        y = xc * lax.rsqrt(var + LN_EPS) * gam + beta
        o_ref[r * CONV_ROWS:(r + 1) * CONV_ROWS, :] = (y * _sigmoid(y)).astype(o_ref.dtype)


def _conformer_conv(proj, w, b, g, beta, batch, seq, ts):
    t = proj.shape[0]
    nsb = seq // ts
    cb = COL_CU // (2 * CONV_CH)
    full = lambda a: pl.BlockSpec(a.shape, lambda bb, s: (0,) * a.ndim)
    return pl.pallas_call(
        functools.partial(_conv_kernel, ts=ts),
        grid=(batch, nsb),
        in_specs=[pl.BlockSpec((ts, 2 * CONV_CH), lambda bb, s: (bb * nsb + s, cb)),
                  full(w), full(b), full(g), full(beta)],
        out_specs=pl.BlockSpec((ts, CONV_CH), lambda bb, s: (bb * nsb + s, 0)),
        out_shape=jax.ShapeDtypeStruct((t, CONV_CH), BF16),
        scratch_shapes=[pltpu.VMEM((CONV_HALO + ts, CONV_CH), F32)],
        compiler_params=_cparams(("parallel", "arbitrary")),
        name="conformer_conv",
    )(proj, w, b, g, beta)


def _sb_attn_kernel(q_ref, k_ref, v_ref, o_ref, *, tq):
    i = pl.program_id(2)
    lane = lax.broadcasted_iota(jnp.int32, (tq, LANES), 1)
    row = lax.broadcasted_iota(jnp.int32, (tq, tq), 0)
    col = lax.broadcasted_iota(jnp.int32, (tq, tq), 1)
    strict = col < row
    suffix = (row > col).astype(BF16)
    ones = jnp.ones((tq, LANES), BF16)
    q = q_ref[...]
    scale = SB_HEAD_DIM ** -0.5
    outs = []
    for hh in range(2):
        head_lanes = (lane >= hh * SB_HEAD_DIM) & (lane < (hh + 1) * SB_HEAD_DIM)
        qh = jnp.where(head_lanes, q * scale, 0.0).astype(BF16)

        def step(kc, carry, masked):
            run, acc = carry
            k0 = pl.multiple_of(kc * tq, tq)
            kb = k_ref[pl.ds(k0, tq), :]
            vb = v_ref[pl.ds(k0, tq), :]
            z = _dot_nt(qh, kb)
            e = jnp.exp(-jnp.abs(z))
            l1pe = jnp.log(1.0 + e)
            sp = jnp.maximum(z, 0.0) + l1pe
            log_beta = jnp.minimum(z, 0.0) - l1pe
            if masked:
                sp = jnp.where(strict, sp, 0.0)
            hi, lo = _split_bf16(sp)
            excl = _dot(hi, suffix) + _dot(lo, suffix)
            tot = _dot(hi, ones) + _dot(lo, ones)
            wgt = jnp.exp(log_beta - excl - run[:, :1])
            if masked:
                wgt = jnp.where(strict, wgt, 0.0)
            acc = acc + _dot(wgt.astype(BF16), vb)
            return run + tot, acc

        init = (jnp.zeros((tq, LANES), F32), jnp.zeros((tq, LANES), F32))
        carry = step(i, init, True)
        _, acc = lax.fori_loop(0, i, lambda it, cr: step(i - 1 - it, cr, False), carry)
        outs.append(acc)
    o_ref[...] = jnp.where(lane < SB_HEAD_DIM, outs[0], outs[1]).astype(o_ref.dtype)


def _sb_attn(proj, batch, seq, tq):
    t = proj.shape[0]
    nqb = seq // tq
    qc, kc, vc = (COL_SB // LANES, (COL_SB + 256) // LANES, (COL_SB + 512) // LANES)
    return pl.pallas_call(
        functools.partial(_sb_attn_kernel, tq=tq),
        grid=(batch, SB_HEADS // 2, nqb),
        in_specs=[pl.BlockSpec((tq, LANES), lambda b, p, i: (b * nqb + i, qc + p)),
                  pl.BlockSpec((seq, LANES), lambda b, p, i: (b, kc + p)),
                  pl.BlockSpec((seq, LANES), lambda b, p, i: (b, vc + p))],
        out_specs=pl.BlockSpec((tq, LANES), lambda b, p, i: (b * nqb + i, p)),
        out_shape=jax.ShapeDtypeStruct((t, SB_HEADS * SB_HEAD_DIM), BF16),
        compiler_params=_cparams(("parallel", "parallel", "arbitrary")),
        name="sb_attn",
    )(proj, proj, proj)


def _merge_kernel(x_ref, gl_ref, ya_ref, yb_ref, yc_ref, yd_ref, wb_ref, wo_ref, o_ref):
    merged = None
    for n, y_ref in enumerate((ya_ref, yb_ref, yc_ref, yd_ref)):
        br = _dot(y_ref[...], wb_ref[n])
        gate = _sigmoid(gl_ref[:, n * D_MODEL:(n + 1) * D_MODEL].astype(F32))
        merged = gate * br if merged is None else merged + gate * br
    o_ref[...] = x_ref[...] + _dot(merged.astype(BF16), wo_ref[...])


def _merge(x2d, proj, ya, yb, yc, yd, wb, wo, tm):
    t, d = x2d.shape
    yspec = pl.BlockSpec((tm, BRANCH_W), lambda i: (i, 0))
    return pl.pallas_call(
        _merge_kernel,
        grid=(t // tm,),
        in_specs=[pl.BlockSpec((tm, d), lambda i: (i, 0)),
                  pl.BlockSpec((tm, N_BRANCH * d), lambda i: (i, COL_GATE // (N_BRANCH * d))),
                  yspec, yspec, yspec, yspec,
                  pl.BlockSpec(wb.shape, lambda i: (0, 0, 0)),
                  pl.BlockSpec(wo.shape, lambda i: (0, 0))],
        out_specs=pl.BlockSpec((tm, d), lambda i: (i, 0)),
        out_shape=jax.ShapeDtypeStruct((t, d), F32),
        compiler_params=_cparams(("parallel",)),
        name="merge_out",
    )(x2d, proj, ya, yb, yc, yd, wb, wo)


FFN_HALO = 16


def _ffn_kernel(x_ref, xh_ref, g_ref, wv_ref, wg_ref, cwv_ref, cwg_ref, cbv_ref, cbg_ref,
                wd_ref, o_ref, h_ref, acc_ref, *, tm, blocks_per_seq):
    i = pl.program_id(0)
    c = pl.program_id(1)

    def norm(x):
        ms = jnp.mean(x * x, axis=-1, keepdims=True)
        return (x * lax.rsqrt(ms + EPS) * g_ref[...]).astype(BF16)

    @pl.when(c == 0)
    def _():
        h_ref[FFN_HALO:, :] = norm(x_ref[...])
        keep = (i % blocks_per_seq != 0).astype(F32)
        h_ref[0:FFN_HALO, :] = norm(xh_ref[...] * keep)
        acc_ref[...] = jnp.zeros_like(acc_ref)

    h = h_ref[...]

    def conv(u, cw_ref, cb_ref):
        cw = cw_ref[...]
        u1 = pltpu.roll(u, 1, 0)[FFN_HALO:]
        u2 = pltpu.roll(u, 2, 0)[FFN_HALO:]
        return cw[2:3] * u[FFN_HALO:] + cw[1:2] * u1 + cw[0:1] * u2 + cb_ref[...]

    val = conv(_dot(h, wv_ref[...]), cwv_ref, cbv_ref)
    gate = conv(_dot(h, wg_ref[...]), cwg_ref, cbg_ref)
    act = (gate * _sigmoid(gate) * val).astype(BF16)
    acc_ref[...] += _dot(act, wd_ref[...])

    @pl.when(c == pl.num_programs(1) - 1)
    def _():
        o_ref[...] = x_ref[...] + acc_ref[...]


def _ffn(x2d, g, w_up, cw, cb, w_down, seq, tm, tk):
    t, d = x2d.shape
    nk = D_FF // tk
    hb = tm // FFN_HALO
    return pl.pallas_call(
        functools.partial(_ffn_kernel, tm=tm, blocks_per_seq=seq // tm),
        grid=(t // tm, nk),
        in_specs=[pl.BlockSpec((tm, d), lambda i, c: (i, 0)),
                  pl.BlockSpec((FFN_HALO, d), lambda i, c: (jnp.maximum(i * hb - 1, 0), 0)),
                  pl.BlockSpec((1, d), lambda i, c: (0, 0)),
                  pl.BlockSpec((d, tk), lambda i, c: (0, c)),
                  pl.BlockSpec((d, tk), lambda i, c: (0, nk + c)),
                  pl.BlockSpec((FFN_CONV_K, tk), lambda i, c: (0, c)),
                  pl.BlockSpec((FFN_CONV_K, tk), lambda i, c: (0, nk + c)),
                  pl.BlockSpec((1, tk), lambda i, c: (0, c)),
                  pl.BlockSpec((1, tk), lambda i, c: (0, nk + c)),
                  pl.BlockSpec((tk, d), lambda i, c: (c, 0))],
        out_specs=pl.BlockSpec((tm, d), lambda i, c: (i, 0)),
        out_shape=jax.ShapeDtypeStruct((t, d), F32),
        scratch_shapes=[pltpu.VMEM((tm + FFN_HALO, d), BF16), pltpu.VMEM((tm, d), F32)],
        compiler_params=_cparams(("parallel", "arbitrary")),
        name="ffn",
    )(x2d, x2d, g, w_up, w_up, cw, cw, cb, cb, w_down)


def _prep_w_in(w_in):
    offs = np.concatenate([[0], np.cumsum(IN_COLS)])
    seg = lambda i: w_in[..., offs[i]:offs[i + 1]]
    pad = jnp.zeros(w_in.shape[:-1] + (MLA_W - MLA_Q_RANK - MLA_KV_RANK - MLA_ROPE,), w_in.dtype)
    parts = [seg(11), seg(0), seg(1), seg(2), seg(3), seg(4), seg(5), seg(6), pad,
             seg(7), seg(8), seg(9), seg(10)]
    return jnp.concatenate(parts, axis=-1).astype(BF16)


def _prep_mla_weights(w_uq, w_ukv):
    nl = w_uq.shape[0]
    wq = w_uq.reshape(nl, MLA_Q_RANK, MLA_HEADS, MLA_QK)
    wq = jnp.pad(wq, ((0, 0), (0, 0), (0, 0), (0, HEAD_PAD - MLA_QK)))
    wq = wq.reshape(nl, MLA_Q_RANK, MLA_HEADS * HEAD_PAD)
    wkv = w_ukv.reshape(nl, MLA_KV_RANK, MLA_HEADS, MLA_NOPE + MLA_V)
    wk = jnp.pad(wkv[..., :MLA_NOPE], ((0, 0), (0, 0), (0, 0), (0, HEAD_PAD - MLA_NOPE)))
    wk = wk.reshape(nl, MLA_KV_RANK, MLA_HEADS * HEAD_PAD)
    wv = wkv[..., MLA_NOPE:].reshape(nl, MLA_KV_RANK, MLA_HEADS * MLA_V)
    place = np.zeros((LANES, MLA_HEADS * HEAD_PAD), np.float32)
    for h in range(MLA_HEADS):
        for r in range(MLA_ROPE):
            place[r, h * HEAD_PAD + MLA_NOPE + r] = 1.0
    wkc = jnp.concatenate([wk, jnp.broadcast_to(jnp.asarray(place), (nl,) + place.shape)], axis=1)
    return wq.astype(BF16), wkc.astype(BF16), wv.astype(BF16)


def _rope_tables(seq):
    half = MLA_ROPE // 2
    freqs = ROPE_BASE ** (-jnp.arange(half, dtype=F32) / half)
    ang = jnp.arange(seq, dtype=F32)[:, None] * freqs[None, :]
    cos, sin = jnp.cos(ang), jnp.sin(ang)
    ones = jnp.ones((seq, MLA_NOPE), F32)
    zeros = lambda n: jnp.zeros((seq, n), F32)
    tail = HEAD_PAD - MLA_QK
    cos_t = jnp.concatenate([ones, cos, cos, jnp.ones((seq, tail), F32)], axis=1)
    s1_t = jnp.concatenate([zeros(MLA_NOPE + half), sin, zeros(tail)], axis=1)
    s2_t = jnp.concatenate([zeros(MLA_NOPE), -sin, zeros(half + tail)], axis=1)
    return cos_t, s1_t, s2_t


def _pad_lanes(a, n):
    return jnp.pad(a, ((0, 0), (0, n - a.shape[-1])))


def kernel(x, g_mix, w_in, lb_logits, g_hg_out, g_q_lat, w_uq, g_kv_lat, w_ukv, g_qk_q, g_qk_k,
           conv_w, conv_b, conv_ln_g, conv_ln_b, w_branch, w_out, g_ffn, w_up, ffn_conv_w,
           ffn_conv_b, w_down):
    batch, seq, d = x.shape
    depth = w_in.shape[0]
    t = batch * seq

    tm_proj = min(1024, seq)
    ts_hg = min(512, seq)
    tm_mla = min(512, seq)
    tq_attn = 256
    ts_conv = min(512, seq)
    tm_merge = min(512, seq)
    tm_ffn = min(1024, seq)

    w_in_p = _prep_w_in(w_in)
    wq_p, wkc_p, wv_p = _prep_mla_weights(w_uq, w_ukv)
    cos_t, s1_t, s2_t = _rope_tables(seq)
    gqq_p = _pad_lanes(g_qk_q, HEAD_PAD)
    gqk_p = _pad_lanes(g_qk_k, HEAD_PAD)
    gout_t = jnp.tile(g_hg_out, (1, HG_HEADS))
    w_branch_b = w_branch.astype(BF16)
    w_out_b = w_out.astype(BF16)
    w_up_b = w_up.astype(BF16)
    w_down_b = w_down.astype(BF16)
    layer_ids = jnp.arange(depth)[:, None]
    lb_logits = lb_logits.astype(F32)

    x2d = x.reshape(t, d)
    for l in range(depth):
        row = lambda a: a[l][None, :]
        proj = _norm_matmul(x2d, row(g_mix), w_in_p[l], tm_proj, 768)
        lmask = ((layer_ids >= 1) & (layer_ids <= l)).astype(F32)
        y_a = _hgrn(proj, lb_logits, lmask, row(gout_t), batch, seq, ts_hg)
        q, k, v = _mla_prep(proj, row(g_q_lat), wq_p[l], row(g_kv_lat), wkc_p[l], wv_p[l],
                            row(gqq_p), row(gqk_p), cos_t, s1_t, s2_t, seq, tm_mla)
        y_b = _mla_attn(q, k, v, batch, seq, tq_attn)
        y_c = _conformer_conv(proj, conv_w[l], row(conv_b), row(conv_ln_g), row(conv_ln_b),
                              batch, seq, ts_conv)
        y_d = _sb_attn(proj, batch, seq, tq_attn)
        x2d = _merge(x2d, proj, y_a, y_b, y_c, y_d, w_branch_b[l], w_out_b[l], tm_merge)
        x2d = _ffn(x2d, row(g_ffn), w_up_b[l], ffn_conv_w[l], row(ffn_conv_b), w_down_b[l],
                   seq, tm_ffn, 256)
    return x2d.reshape(batch, seq, d)
```

```python
import functools

import jax
import jax.numpy as jnp
import numpy as np
from jax import lax
from jax.experimental import pallas as pl
from jax.experimental.pallas import tpu as pltpu

F32 = jnp.float32
BF16 = jnp.bfloat16

D_MODEL = 1024
HG_HEADS = 4
HG_DK = 64
HG_DV = 64
MLA_HEADS = 4
MLA_Q_RANK = 256
MLA_KV_RANK = 128
MLA_NOPE = 64
MLA_ROPE = 32
MLA_V = 64
MLA_QK = MLA_NOPE + MLA_ROPE
ROPE_BASE = 10000.0
CONV_CH = 256
CONV_K = 31
SB_HEADS = 4
SB_HEAD_DIM = 64
N_BRANCH = 4
BRANCH_W = 256
D_FF = 2816
FFN_CONV_K = 3
EPS = 1e-6
LN_EPS = 1e-5
LOG2E = 1.4426950408889634

IN_COLS = (256, 256, 256, 256, MLA_Q_RANK, MLA_KV_RANK, MLA_ROPE, 2 * CONV_CH, 256, 256, 256,
           N_BRANCH * D_MODEL)

LANES = 128
BF16_ROWS = 16
SUBLANES_F32 = 8
VMEM_LIMIT = 56 * 1024 * 1024

COL_GATE = 0
COL_HG = 4096
COL_MLA = 5120
COL_CU = 5632
COL_SB = 6144
N_IN = 6912
MLA_W = 512
HEAD_PAD = 128

HG_CHUNK = 64
HG_SUB = 16
HG_UNROLL = 4
HG_W = HG_HEADS * HG_DK


def _cparams(sem):
    return pltpu.CompilerParams(dimension_semantics=sem, vmem_limit_bytes=VMEM_LIMIT)


def _sigmoid(x):
    return 1.0 / (1.0 + jnp.exp(-x))


def _split_bf16(x):
    hi = x.astype(BF16)
    lo = (x - hi.astype(F32)).astype(BF16)
    return hi, lo


def _dot(a, b):
    return jnp.dot(a, b, preferred_element_type=F32)


def _dot_nt(a, b):
    return lax.dot_general(a, b, (((1,), (1,)), ((), ())), preferred_element_type=F32)


def _dot_tn(a, b):
    return lax.dot_general(a, b, (((0,), (0,)), ((), ())), preferred_element_type=F32)


def _norm_matmul_kernel(x_ref, g_ref, w_ref, o_ref, h_ref):
    @pl.when(pl.program_id(1) == 0)
    def _():
        x = x_ref[...]
        ms = jnp.mean(x * x, axis=-1, keepdims=True)
        h_ref[...] = (x * lax.rsqrt(ms + EPS) * g_ref[...]).astype(BF16)

    o_ref[...] = _dot(h_ref[...], w_ref[...]).astype(o_ref.dtype)


def _layer_spec(a, l):
    nd = a.ndim - 1
    return pl.BlockSpec((None,) + a.shape[1:], lambda *_: (l,) + (0,) * nd)


def _norm_matmul(x2d, g, w, l, tm, tn):
    t, d = x2d.shape
    n = w.shape[2]
    return pl.pallas_call(
        _norm_matmul_kernel,
        grid=(t // tm, n // tn),
        in_specs=[pl.BlockSpec((tm, d), lambda i, j: (i, 0)),
                  _layer_spec(g, l),
                  pl.BlockSpec((None, d, tn), lambda i, j: (l, 0, j))],
        out_specs=pl.BlockSpec((tm, tn), lambda i, j: (i, j)),
        out_shape=jax.ShapeDtypeStruct((t, n), BF16),
        scratch_shapes=[pltpu.VMEM((tm, d), BF16)],
        compiler_params=_cparams(("parallel", "arbitrary")),
        name="norm_matmul",
    )(x2d, g, w)


def _hgrn_kernel(p_ref, lbl_ref, lmask_ref, gout_ref, o_ref, st_ref, *, n_chunks):
    @pl.when(pl.program_id(1) == 0)
    def _():
        st_ref[...] = jnp.zeros_like(st_ref)

    c = HG_CHUNK
    w = HG_W
    logits = lbl_ref[...]
    mx = jnp.max(logits, axis=0, keepdims=True)
    ex = jnp.exp(logits - mx)
    sm = ex / jnp.sum(ex, axis=0, keepdims=True)
    lb = jnp.sum(sm * lmask_ref[...], axis=0, keepdims=True)
    log_lb = jnp.log(lb)
    log_1mlb = jnp.log1p(-lb)
    one_mlb = 1.0 - lb
    gout = gout_ref[...]

    row = lax.broadcasted_iota(jnp.int32, (c, c), 0)
    col = lax.broadcasted_iota(jnp.int32, (c, c), 1)
    tri = (col <= row).astype(BF16)
    rmod = lax.broadcasted_iota(jnp.int32, (c, w), 0) % HG_SUB
    rmod8 = lax.broadcasted_iota(jnp.int32, (c, w), 0) % SUBLANES_F32
    blk_t = lax.broadcasted_iota(jnp.int32, (HG_HEADS * c, c), 0) % c // HG_SUB
    blk_s = lax.broadcasted_iota(jnp.int32, (HG_HEADS * c, c), 1) // HG_SUB
    same_block = blk_t == blk_s
    lane_head = lax.broadcasted_iota(jnp.int32, (c, w), 1) // HG_DK
    eh_r = lax.broadcasted_iota(jnp.int32, (w, w), 0) // HG_DK
    eh_c = lax.broadcasted_iota(jnp.int32, (w, w), 1) // HG_DK
    same_head = eh_r == eh_c
    eh = same_head.astype(BF16)
    n_sub = c // HG_SUB
    cat_head = lax.broadcasted_iota(jnp.int32, (c, (n_sub - 1) * w), 1) % w // HG_DK

    def chunk(ci, carry):
        r0 = pl.multiple_of(ci * c, c)
        blk = p_ref[pl.ds(r0, c), :].astype(F32)
        hq, hf, hv, hg = blk[:, :w], blk[:, w:2 * w], blk[:, 2 * w:3 * w], blk[:, 3 * w:]
        q = hq * _sigmoid(hq)
        e = jnp.exp(-jnp.abs(hf))
        l1pe = jnp.log(1.0 + e)
        logsig = jnp.minimum(hf, 0.0) - l1pe
        b_term = log_1mlb + logsig
        mxab = jnp.maximum(log_lb, b_term)
        lf = mxab + jnp.log(1.0 + jnp.exp(-jnp.abs(log_lb - b_term)))
        kk = one_mlb * jnp.where(hf >= 0.0, e, 1.0) / (1.0 + e)
        v_bf = hv.astype(BF16)

        lf_hi, lf_lo = _split_bf16(lf)
        bcum = _dot(tri, lf_hi) + _dot(tri, lf_lo)
        b_last = bcum[c - 1:c, :]

        st = st_ref[...]
        qe = (q * jnp.exp(bcum)).astype(BF16)
        o = _dot_nt(qe, st.astype(BF16))

        qms, kms = [], []
        for i in range(1, n_sub):
            lo_r, hi_r = i * HG_SUB, (i + 1) * HG_SUB
            ref_row = bcum[lo_r - 1:lo_r, :]
            qi = q[lo_r:hi_r] * jnp.exp(bcum[lo_r:hi_r] - ref_row)
            pieces = [jnp.zeros((lo_r, w), F32), qi]
            if hi_r < c:
                pieces.append(jnp.zeros((c - hi_r, w), F32))
            qms.append(jnp.concatenate(pieces, axis=0))
            ki = kk[:lo_r] * jnp.exp(ref_row - bcum[:lo_r])
            kms.append(jnp.concatenate([ki, jnp.zeros((c - lo_r, w), F32)], axis=0))
        qc = jnp.concatenate(qms, axis=1)
        kc = jnp.concatenate(kms, axis=1).astype(BF16)
        qstack = jnp.concatenate(
            [jnp.where(cat_head == h, qc, 0.0) for h in range(HG_HEADS)], axis=0).astype(BF16)
        a_stack = _dot_nt(qstack, kc)

        half = HG_SUB // 2
        ref2 = jnp.concatenate(
            [jnp.broadcast_to(bcum[i * HG_SUB + half - 1:i * HG_SUB + half, :], (HG_SUB, w))
             for i in range(n_sub)], axis=0)
        upper = rmod >= half
        q2 = jnp.where(upper, q * jnp.exp(jnp.where(upper, bcum - ref2, 0.0)), 0.0)
        k2 = jnp.where(upper, 0.0, kk * jnp.exp(jnp.where(upper, 0.0, ref2 - bcum))).astype(BF16)
        q2stack = jnp.concatenate(
            [jnp.where(lane_head == h, q2, 0.0) for h in range(HG_HEADS)], axis=0).astype(BF16)
        a2 = _dot_nt(q2stack, k2)
        a_tot = a_stack + jnp.where(same_block, a2, 0.0)
        r_full = _dot(a_tot.astype(BF16), v_bf)
        for h in range(HG_HEADS):
            o = o + jnp.where(lane_head == h, r_full[h * c:(h + 1) * c], 0.0)

        def shift(a, d):
            a3 = a.reshape(c // SUBLANES_F32, SUBLANES_F32, w)
            return pltpu.roll(a3, d, 1).reshape(c, w)

        for d in range(half):
            if d == 0:
                ks, bs, vs = kk, bcum, hv
            else:
                ks, bs, vs = shift(kk, d), shift(bcum, d), shift(hv, d)
            valid = rmod8 >= d
            pd = jnp.where(valid, q * ks * jnp.exp(jnp.where(valid, bcum - bs, 0.0)), 0.0)
            o = o + _dot(pd.astype(BF16), eh) * vs

        kdec = (kk * jnp.exp(b_last - bcum)).astype(BF16)
        upd = _dot_tn(v_bf, kdec)
        st_ref[...] = st * jnp.exp(b_last) + jnp.where(same_head, upd, 0.0)

        o2_hi, o2_lo = _split_bf16(o * o)
        ms = (_dot(o2_hi, eh) + _dot(o2_lo, eh)) * (1.0 / HG_DV)
        y = o * lax.rsqrt(ms + EPS) * gout * (hg * _sigmoid(hg))
        o_ref[pl.ds(r0, c), :] = y.astype(o_ref.dtype)
        return carry

    def trip(ti, carry):
        for u in range(HG_UNROLL):
            carry = chunk(ti * HG_UNROLL + u, carry)
        return carry

    lax.fori_loop(0, n_chunks // HG_UNROLL, trip, 0)


def _hgrn(proj, lb_logits, lmasks, gout_t, l, batch, seq, ts):
    t = proj.shape[0]
    nsb = seq // ts
    cb = COL_HG // (4 * HG_W)
    return pl.pallas_call(
        functools.partial(_hgrn_kernel, n_chunks=ts // HG_CHUNK),
        grid=(batch, nsb),
        in_specs=[pl.BlockSpec((ts, 4 * HG_W), lambda b, s: (b * nsb + s, cb)),
                  pl.BlockSpec(lb_logits.shape, lambda b, s: (0, 0)),
                  _layer_spec(lmasks, l),
                  _layer_spec(gout_t, l)],
        out_specs=pl.BlockSpec((ts, HG_W), lambda b, s: (b * nsb + s, 0)),
        out_shape=jax.ShapeDtypeStruct((t, HG_W), BF16),
        scratch_shapes=[pltpu.VMEM((HG_W, HG_W), F32)],
        compiler_params=_cparams(("parallel", "arbitrary")),
        name="hgrn2",
    )(proj, lb_logits, lmasks, gout_t)


def _mla_prep_kernel(p_ref, gq_ref, wuq_ref, gkv_ref, wkc_ref, wuv_ref, gqq_ref, gqk_ref,
                     cos_ref, s1_ref, s2_ref, q_ref, k_ref, v_ref):
    blk = p_ref[...].astype(F32)
    cq = blk[:, :MLA_Q_RANK]
    ckv = blk[:, MLA_Q_RANK:MLA_Q_RANK + MLA_KV_RANK]
    krp = blk[:, MLA_Q_RANK + MLA_KV_RANK:]

    def rms(x, g):
        ms = jnp.mean(x * x, axis=-1, keepdims=True)
        return x * lax.rsqrt(ms + EPS) * g

    cqn = rms(cq, gq_ref[...]).astype(BF16)
    ckvn = rms(ckv, gkv_ref[...]).astype(BF16)
    q_raw = _dot(cqn, wuq_ref[...])
    k_raw = _dot(jnp.concatenate([ckvn, krp.astype(BF16)], axis=1), wkc_ref[...])
    v_ref[...] = _dot(ckvn, wuv_ref[...]).astype(v_ref.dtype)

    cos, s1, s2 = cos_ref[...], s1_ref[...], s2_ref[...]

    def head(x, g, scale):
        ms = jnp.sum(x * x, axis=-1, keepdims=True) * (1.0 / MLA_QK)
        xn = x * lax.rsqrt(ms + EPS) * g
        half = MLA_ROPE // 2
        y = xn * cos + pltpu.roll(xn, half, 1) * s1 + pltpu.roll(xn, HEAD_PAD - half, 1) * s2
        return y * scale

    for h in range(MLA_HEADS):
        sl = slice(h * HEAD_PAD, (h + 1) * HEAD_PAD)
        q_ref[:, sl] = head(q_raw[:, sl], gqq_ref[...], MLA_QK ** -0.5 * LOG2E).astype(q_ref.dtype)
        k_ref[:, sl] = head(k_raw[:, sl], gqk_ref[...], 1.0).astype(k_ref.dtype)


def _mla_prep(proj, gq, wuq, gkv, wkc, wuv, gqq, gqk, cos_t, s1_t, s2_t, l, seq, tm):
    t = proj.shape[0]
    nsb = seq // tm
    cb = COL_MLA // MLA_W
    full = lambda a: _layer_spec(a, l)
    tab = pl.BlockSpec((tm, HEAD_PAD), lambda i: (i % nsb, 0))
    qk_w = MLA_HEADS * HEAD_PAD
    return pl.pallas_call(
        _mla_prep_kernel,
        grid=(t // tm,),
        in_specs=[pl.BlockSpec((tm, MLA_W), lambda i: (i, cb)),
                  full(gq), full(wuq), full(gkv), full(wkc), full(wuv), full(gqq), full(gqk),
                  tab, tab, tab],
        out_specs=[pl.BlockSpec((tm, qk_w), lambda i: (i, 0)),
                   pl.BlockSpec((tm, qk_w), lambda i: (i, 0)),
                   pl.BlockSpec((tm, MLA_HEADS * MLA_V), lambda i: (i, 0))],
        out_shape=[jax.ShapeDtypeStruct((t, qk_w), BF16),
                   jax.ShapeDtypeStruct((t, qk_w), BF16),
                   jax.ShapeDtypeStruct((t, MLA_HEADS * MLA_V), BF16)],
        compiler_params=_cparams(("parallel",)),
        name="mla_prep",
    )(proj, gq, wuq, gkv, wkc, wuv, gqq, gqk, cos_t, s1_t, s2_t)


def _mla_attn_kernel(q_ref, k_ref, v_ref, o_ref, *, tq, tk):
    i = pl.program_id(2)
    n_diag = tq // tk
    lane = lax.broadcasted_iota(jnp.int32, (tq, LANES), 1)
    row = lax.broadcasted_iota(jnp.int32, (tq, tk), 0)
    col = lax.broadcasted_iota(jnp.int32, (tq, tk), 1)
    qs = [q_ref[:, hh * HEAD_PAD:(hh + 1) * HEAD_PAD] for hh in range(2)]

    def step(kc, carry, dj):
        k0 = pl.multiple_of(kc * tk, tk)
        vb = v_ref[pl.ds(k0, tk), :]
        out = []
        for hh in range(2):
            m, l, acc = carry[hh]
            kb = k_ref[pl.ds(k0, tk), hh * HEAD_PAD:(hh + 1) * HEAD_PAD]
            s = _dot_nt(qs[hh], kb)
            if dj is not None:
                s = jnp.where(col + dj * tk <= row, s, -jnp.inf)
            m_new = jnp.maximum(m, jnp.max(s, axis=-1, keepdims=True))
            alpha = jnp.exp2(m - m_new)
            p = jnp.exp2(s - m_new)
            l = alpha * l + jnp.sum(p, axis=-1, keepdims=True)
            acc = alpha * acc + _dot(p.astype(BF16), vb)
            out.append((m_new, l, acc))
        return tuple(out)

    one = (jnp.full((tq, 1), -jnp.inf, F32), jnp.zeros((tq, 1), F32), jnp.zeros((tq, LANES), F32))
    carry = step(i * n_diag, (one, one), 0)
    for dj in range(1, n_diag):
        carry = step(i * n_diag + dj, carry, dj)
    carry = lax.fori_loop(0, i * n_diag, lambda kc, cr: step(kc, cr, None), carry)
    outs = [acc / l for (_, l, acc) in carry]
    o_ref[...] = jnp.where(lane < MLA_V, outs[0], outs[1]).astype(o_ref.dtype)


def _mla_attn(q, k, v, batch, seq, tq, tk):
    t = q.shape[0]
    nqb = seq // tq
    return pl.pallas_call(
        functools.partial(_mla_attn_kernel, tq=tq, tk=tk),
        grid=(batch, MLA_HEADS // 2, nqb),
        in_specs=[pl.BlockSpec((tq, 2 * HEAD_PAD), lambda b, p, i: (b * nqb + i, p)),
                  pl.BlockSpec((seq, 2 * HEAD_PAD), lambda b, p, i: (b, p)),
                  pl.BlockSpec((seq, LANES), lambda b, p, i: (b, p))],
        out_specs=pl.BlockSpec((tq, LANES), lambda b, p, i: (b * nqb + i, p)),
        out_shape=jax.ShapeDtypeStruct((t, MLA_HEADS * MLA_V), BF16),
        compiler_params=_cparams(("parallel", "parallel", "arbitrary")),
        name="mla_attn",
    )(q, k, v)


CONV_HALO = 32
CONV_ROWS = 64


SUBLANES = SUBLANES_F32


def _conv_kernel(p_ref, w_ref, b_ref, g_ref, beta_ref, o_ref, halo_ref, v_ref, *, ts):
    @pl.when(pl.program_id(1) == 0)
    def _():
        halo_ref[...] = jnp.zeros_like(halo_ref)

    blk = p_ref[...].astype(F32)
    u = blk[:, :CONV_CH] * _sigmoid(blk[:, CONV_CH:])
    u_ext = jnp.concatenate([halo_ref[...], u], axis=0)
    halo_ref[...] = u[ts - CONV_HALO:]
    n_ext = CONV_HALO + ts
    v_ref[0] = u_ext
    for m in range(1, SUBLANES):
        v_ref[m] = pltpu.roll(u_ext, n_ext - m, 0)
    w = w_ref[...]
    bias = b_ref[...]
    gam, beta = g_ref[...], beta_ref[...]
    first = CONV_HALO - (CONV_K - 1)
    for r in range(ts // CONV_ROWS):
        acc = jnp.zeros((CONV_ROWS, CONV_CH), F32) + bias
        for j in range(CONV_K):
            off = first + j
            start = r * CONV_ROWS + (off // SUBLANES) * SUBLANES
            acc = acc + w[j:j + 1, :] * v_ref[off % SUBLANES, start:start + CONV_ROWS, :]
        mu = jnp.mean(acc, axis=-1, keepdims=True)
        xc = acc - mu
        var = jnp.mean(xc * xc, axis=-1, keepdims=True)
        y = xc * lax.rsqrt(var + LN_EPS) * gam + beta
        o_ref[r * CONV_ROWS:(r + 1) * CONV_ROWS, :] = (y * _sigmoid(y)).astype(o_ref.dtype)


def _conformer_conv(proj, w, b, g, beta, l, batch, seq, ts):
    t = proj.shape[0]
    nsb = seq // ts
    cb = COL_CU // (2 * CONV_CH)
    full = lambda a: _layer_spec(a, l)
    return pl.pallas_call(
        functools.partial(_conv_kernel, ts=ts),
        grid=(batch, nsb),
        in_specs=[pl.BlockSpec((ts, 2 * CONV_CH), lambda bb, s: (bb * nsb + s, cb)),
                  full(w), full(b), full(g), full(beta)],
        out_specs=pl.BlockSpec((ts, CONV_CH), lambda bb, s: (bb * nsb + s, 0)),
        out_shape=jax.ShapeDtypeStruct((t, CONV_CH), BF16),
        scratch_shapes=[pltpu.VMEM((CONV_HALO, CONV_CH), F32),
                        pltpu.VMEM((SUBLANES, CONV_HALO + ts, CONV_CH), F32)],
        compiler_params=_cparams(("parallel", "arbitrary")),
        name="conformer_conv",
    )(proj, w, b, g, beta)


SB_SUB = 256


def _sb_attn_kernel(q_ref, k_ref, v_ref, o_ref, *, tq, tk):
    i = pl.program_id(2)
    n_diag = tq // tk
    n_sub = tk // SB_SUB
    lane = lax.broadcasted_iota(jnp.int32, (tq, LANES), 1)
    row = lax.broadcasted_iota(jnp.int32, (tq, SB_SUB), 0)
    col = lax.broadcasted_iota(jnp.int32, (tq, SB_SUB), 1)
    r2 = lax.broadcasted_iota(jnp.int32, (SB_SUB, SB_SUB), 0)
    c2 = lax.broadcasted_iota(jnp.int32, (SB_SUB, SB_SUB), 1)
    suffix = (r2 >= c2).astype(BF16)
    q = q_ref[...]
    qs = [jnp.where((lane >= hh * SB_HEAD_DIM) & (lane < (hh + 1) * SB_HEAD_DIM), q,
                    jnp.zeros_like(q)) for hh in range(2)]

    def step(kc, carry, dj):
        k0 = pl.multiple_of(kc * tk, tk)
        kb = k_ref[pl.ds(k0, tk), :]
        vb = v_ref[pl.ds(k0, tk), :]
        out = []
        for hh in range(2):
            run, acc = carry[hh]
            z_all = _dot_nt(qs[hh], kb)
            wgts = [None] * n_sub
            for sub in reversed(range(n_sub)):
                z = z_all[:, sub * SB_SUB:(sub + 1) * SB_SUB]
                sp = jnp.maximum(z, 0.0) + jnp.log(1.0 + jnp.exp2(-jnp.abs(z))) * LOG2E
                if dj is not None:
                    valid = col + (dj * tk + sub * SB_SUB) < row
                    sp = jnp.where(valid, sp, 0.0)
                hi, lo = _split_bf16(sp)
                incl = _dot(hi, suffix) + _dot(lo, suffix)
                wgt = jnp.exp2(z - incl - run)
                if dj is not None:
                    wgt = jnp.where(valid, wgt, 0.0)
                wgts[sub] = wgt.astype(BF16)
                run = run + incl[:, :1]
            wcat = wgts[0] if n_sub == 1 else jnp.concatenate(wgts, axis=1)
            out.append((run, acc + _dot(wcat, vb)))
        return tuple(out)

    one = (jnp.zeros((tq, 1), F32), jnp.zeros((tq, LANES), F32))
    carry = (one, one)
    for dj in reversed(range(n_diag)):
        carry = step(i * n_diag + dj, carry, dj)
    nfull = i * n_diag
    carry = lax.fori_loop(0, nfull, lambda it, cr: step(nfull - 1 - it, cr, None), carry)
    o_ref[...] = jnp.where(lane < SB_HEAD_DIM, carry[0][1], carry[1][1]).astype(o_ref.dtype)


def _sb_attn(proj, batch, seq, tq, tk):
    t = proj.shape[0]
    nqb = seq // tq
    qc, kc, vc = (COL_SB // LANES, (COL_SB + 256) // LANES, (COL_SB + 512) // LANES)
    return pl.pallas_call(
        functools.partial(_sb_attn_kernel, tq=tq, tk=tk),
        grid=(batch, SB_HEADS // 2, nqb),
        in_specs=[pl.BlockSpec((tq, LANES), lambda b, p, i: (b * nqb + i, qc + p)),
                  pl.BlockSpec((seq, LANES), lambda b, p, i: (b, kc + p)),
                  pl.BlockSpec((seq, LANES), lambda b, p, i: (b, vc + p))],
        out_specs=pl.BlockSpec((tq, LANES), lambda b, p, i: (b * nqb + i, p)),
        out_shape=jax.ShapeDtypeStruct((t, SB_HEADS * SB_HEAD_DIM), BF16),
        compiler_params=_cparams(("parallel", "parallel", "arbitrary")),
        name="sb_attn",
    )(proj, proj, proj)


def _merge_kernel(x_ref, gl_ref, ya_ref, yb_ref, yc_ref, yd_ref, wb_ref, wo_ref, o_ref):
    merged = None
    for n, y_ref in enumerate((ya_ref, yb_ref, yc_ref, yd_ref)):
        br = _dot(y_ref[...], wb_ref[n])
        gate = _sigmoid(gl_ref[:, n * D_MODEL:(n + 1) * D_MODEL].astype(F32))
        merged = gate * br if merged is None else merged + gate * br
    o_ref[...] = x_ref[...] + _dot(merged.astype(BF16), wo_ref[...])


def _merge(x2d, proj, ya, yb, yc, yd, wb, wo, l, tm):
    t, d = x2d.shape
    yspec = pl.BlockSpec((tm, BRANCH_W), lambda i: (i, 0))
    return pl.pallas_call(
        _merge_kernel,
        grid=(t // tm,),
        in_specs=[pl.BlockSpec((tm, d), lambda i: (i, 0)),
                  pl.BlockSpec((tm, N_BRANCH * d), lambda i: (i, COL_GATE // (N_BRANCH * d))),
                  yspec, yspec, yspec, yspec,
                  _layer_spec(wb, l), _layer_spec(wo, l)],
        out_specs=pl.BlockSpec((tm, d), lambda i: (i, 0)),
        out_shape=jax.ShapeDtypeStruct((t, d), F32),
        compiler_params=_cparams(("parallel",)),
        name="merge_out",
    )(x2d, proj, ya, yb, yc, yd, wb, wo)


FFN_HALO = 16


def _ffn_kernel(x_ref, xh_ref, g_ref, wv_ref, wg_ref, cwv_ref, cwg_ref, cbv_ref, cbg_ref,
                wd_ref, o_ref, h_ref, acc_ref, *, tm, blocks_per_seq):
    i = pl.program_id(0)
    c = pl.program_id(1)

    def norm(x):
        ms = jnp.mean(x * x, axis=-1, keepdims=True)
        return (x * lax.rsqrt(ms + EPS) * g_ref[...]).astype(BF16)

    @pl.when(c == 0)
    def _():
        h_ref[FFN_HALO:, :] = norm(x_ref[...])
        keep = (i % blocks_per_seq != 0).astype(F32)
        h_ref[0:FFN_HALO, :] = norm(xh_ref[...] * keep)
        acc_ref[...] = jnp.zeros_like(acc_ref)

    h = h_ref[...]

    def conv(u, cw_ref, cb_ref):
        cw = cw_ref[...]
        u1 = pltpu.roll(u, 1, 0)[FFN_HALO:]
        u2 = pltpu.roll(u, 2, 0)[FFN_HALO:]
        return cw[2:3] * u[FFN_HALO:] + cw[1:2] * u1 + cw[0:1] * u2 + cb_ref[...]

    val = conv(_dot(h, wv_ref[...]), cwv_ref, cbv_ref)
    gate = conv(_dot(h, wg_ref[...]), cwg_ref, cbg_ref)
    act = (gate * _sigmoid(gate) * val).astype(BF16)
    acc_ref[...] += _dot(act, wd_ref[...])

    @pl.when(c == pl.num_programs(1) - 1)
    def _():
        o_ref[...] = x_ref[...] + acc_ref[...]


def _ffn(x2d, g, w_up, cw, cb, w_down, l, seq, tm, tk):
    t, d = x2d.shape
    nk = D_FF // tk
    hb = tm // FFN_HALO
    return pl.pallas_call(
        functools.partial(_ffn_kernel, tm=tm, blocks_per_seq=seq // tm),
        grid=(t // tm, nk),
        in_specs=[pl.BlockSpec((tm, d), lambda i, c: (i, 0)),
                  pl.BlockSpec((FFN_HALO, d), lambda i, c: (jnp.maximum(i * hb - 1, 0), 0)),
                  _layer_spec(g, l),
                  pl.BlockSpec((None, d, tk), lambda i, c: (l, 0, c)),
                  pl.BlockSpec((None, d, tk), lambda i, c: (l, 0, nk + c)),
                  pl.BlockSpec((None, FFN_CONV_K, tk), lambda i, c: (l, 0, c)),
                  pl.BlockSpec((None, FFN_CONV_K, tk), lambda i, c: (l, 0, nk + c)),
                  pl.BlockSpec((None, 1, tk), lambda i, c: (l, 0, c)),
                  pl.BlockSpec((None, 1, tk), lambda i, c: (l, 0, nk + c)),
                  pl.BlockSpec((None, tk, d), lambda i, c: (l, c, 0))],
        out_specs=pl.BlockSpec((tm, d), lambda i, c: (i, 0)),
        out_shape=jax.ShapeDtypeStruct((t, d), F32),
        scratch_shapes=[pltpu.VMEM((tm + FFN_HALO, d), BF16), pltpu.VMEM((tm, d), F32)],
        compiler_params=_cparams(("parallel", "arbitrary")),
        name="ffn",
    )(x2d, x2d, g, w_up, w_up, cw, cw, cb, cb, w_down)


def _prep_w_in(w_in):
    offs = np.concatenate([[0], np.cumsum(IN_COLS)])
    seg = lambda i: w_in[..., offs[i]:offs[i + 1]]
    pad = jnp.zeros(w_in.shape[:-1] + (MLA_W - MLA_Q_RANK - MLA_KV_RANK - MLA_ROPE,), w_in.dtype)
    sq = seg(8) * (SB_HEAD_DIM ** -0.5 * LOG2E)
    parts = [seg(11), seg(0), seg(1), seg(2), seg(3), seg(4), seg(5), seg(6), pad,
             seg(7), sq, seg(9), seg(10)]
    return jnp.concatenate(parts, axis=-1).astype(BF16)


def _prep_mla_weights(w_uq, w_ukv):
    nl = w_uq.shape[0]
    wq = w_uq.reshape(nl, MLA_Q_RANK, MLA_HEADS, MLA_QK)
    wq = jnp.pad(wq, ((0, 0), (0, 0), (0, 0), (0, HEAD_PAD - MLA_QK)))
    wq = wq.reshape(nl, MLA_Q_RANK, MLA_HEADS * HEAD_PAD)
    wkv = w_ukv.reshape(nl, MLA_KV_RANK, MLA_HEADS, MLA_NOPE + MLA_V)
    wk = jnp.pad(wkv[..., :MLA_NOPE], ((0, 0), (0, 0), (0, 0), (0, HEAD_PAD - MLA_NOPE)))
    wk = wk.reshape(nl, MLA_KV_RANK, MLA_HEADS * HEAD_PAD)
    wv = wkv[..., MLA_NOPE:].reshape(nl, MLA_KV_RANK, MLA_HEADS * MLA_V)
    place = np.zeros((LANES, MLA_HEADS * HEAD_PAD), np.float32)
    for h in range(MLA_HEADS):
        for r in range(MLA_ROPE):
            place[r, h * HEAD_PAD + MLA_NOPE + r] = 1.0
    wkc = jnp.concatenate([wk, jnp.broadcast_to(jnp.asarray(place), (nl,) + place.shape)], axis=1)
    return wq.astype(BF16), wkc.astype(BF16), wv.astype(BF16)


def _rope_tables(seq):
    half = MLA_ROPE // 2
    freqs = ROPE_BASE ** (-jnp.arange(half, dtype=F32) / half)
    ang = jnp.arange(seq, dtype=F32)[:, None] * freqs[None, :]
    cos, sin = jnp.cos(ang), jnp.sin(ang)
    ones = jnp.ones((seq, MLA_NOPE), F32)
    zeros = lambda n: jnp.zeros((seq, n), F32)
    tail = HEAD_PAD - MLA_QK
    cos_t = jnp.concatenate([ones, cos, cos, jnp.ones((seq, tail), F32)], axis=1)
    s1_t = jnp.concatenate([zeros(MLA_NOPE + half), sin, zeros(tail)], axis=1)
    s2_t = jnp.concatenate([zeros(MLA_NOPE), -sin, zeros(half + tail)], axis=1)
    return cos_t, s1_t, s2_t


def _pad_lanes(a, n):
    return jnp.pad(a, ((0, 0), (0, n - a.shape[-1])))


def kernel(x, g_mix, w_in, lb_logits, g_hg_out, g_q_lat, w_uq, g_kv_lat, w_ukv, g_qk_q, g_qk_k,
           conv_w, conv_b, conv_ln_g, conv_ln_b, w_branch, w_out, g_ffn, w_up, ffn_conv_w,
           ffn_conv_b, w_down):
    batch, seq, d = x.shape
    depth = w_in.shape[0]
    t = batch * seq

    tm_proj = min(1024, seq)
    ts_hg = min(512, seq)
    tm_mla = min(512, seq)
    tq_attn = min(1024, seq)
    tk_attn = 512
    ts_conv = min(512, seq)
    tm_merge = min(512, seq)
    tm_ffn = min(1024, seq)

    w_in_p = _prep_w_in(w_in)
    wq_p, wkc_p, wv_p = _prep_mla_weights(w_uq, w_ukv)
    cos_t, s1_t, s2_t = _rope_tables(seq)
    rows = lambda a: a[:, None, :]
    gqq_p = rows(_pad_lanes(g_qk_q, HEAD_PAD))
    gqk_p = rows(_pad_lanes(g_qk_k, HEAD_PAD))
    gout_t = rows(jnp.tile(g_hg_out, (1, HG_HEADS)))
    w_branch_b = w_branch.astype(BF16)
    w_out_b = w_out.astype(BF16)
    w_up_b = w_up.astype(BF16)
    w_down_b = w_down.astype(BF16)
    ids = np.arange(depth)
    lmasks = jnp.asarray(((ids[None, :] >= 1) & (ids[None, :] <= ids[:, None]))
                         .astype(np.float32)[:, :, None])
    lb_logits = lb_logits.astype(F32)
    g_mix3, g_ffn3, g_q3, g_kv3 = rows(g_mix), rows(g_ffn), rows(g_q_lat), rows(g_kv_lat)
    conv_b3, ln_g3, ln_b3, ffn_b3 = rows(conv_b), rows(conv_ln_g), rows(conv_ln_b), rows(ffn_conv_b)

    x2d = x.reshape(t, d)
    for l in range(depth):
        proj = _norm_matmul(x2d, g_mix3, w_in_p, l, tm_proj, 768)
        y_a = _hgrn(proj, lb_logits, lmasks, gout_t, l, batch, seq, ts_hg)
        q, k, v = _mla_prep(proj, g_q3, wq_p, g_kv3, wkc_p, wv_p, gqq_p, gqk_p,
                            cos_t, s1_t, s2_t, l, seq, tm_mla)
        y_b = _mla_attn(q, k, v, batch, seq, tq_attn, tk_attn)
        y_c = _conformer_conv(proj, conv_w, conv_b3, ln_g3, ln_b3, l, batch, seq, ts_conv)
        y_d = _sb_attn(proj, batch, seq, tq_attn, tk_attn)
        x2d = _merge(x2d, proj, y_a, y_b, y_c, y_d, w_branch_b, w_out_b, l, tm_merge)
        x2d = _ffn(x2d, g_ffn3, w_up_b, ffn_conv_w, ffn_b3, w_down_b, l, seq, tm_ffn, 256)
    return x2d.reshape(batch, seq, d)
```

```python
import functools

import jax
import jax.numpy as jnp
import numpy as np
from jax import lax
from jax.experimental import pallas as pl
from jax.experimental.pallas import tpu as pltpu

F32 = jnp.float32
BF16 = jnp.bfloat16

D_MODEL = 1024
HG_HEADS = 4
HG_DK = 64
HG_DV = 64
MLA_HEADS = 4
MLA_Q_RANK = 256
MLA_KV_RANK = 128
MLA_NOPE = 64
MLA_ROPE = 32
MLA_V = 64
MLA_QK = MLA_NOPE + MLA_ROPE
ROPE_BASE = 10000.0
CONV_CH = 256
CONV_K = 31
SB_HEADS = 4
SB_HEAD_DIM = 64
N_BRANCH = 4
BRANCH_W = 256
D_FF = 2816
FFN_CONV_K = 3
EPS = 1e-6
LN_EPS = 1e-5
LOG2E = 1.4426950408889634

IN_COLS = (256, 256, 256, 256, MLA_Q_RANK, MLA_KV_RANK, MLA_ROPE, 2 * CONV_CH, 256, 256, 256,
           N_BRANCH * D_MODEL)

LANES = 128
BF16_ROWS = 16
SUBLANES_F32 = 8
VMEM_LIMIT = 56 * 1024 * 1024

COL_GATE = 0
COL_HG = 4096
COL_MLA = 5120
COL_CU = 5632
COL_SB = 6144
N_IN = 6912
MLA_W = 512
HEAD_PAD = 128

HG_CHUNK = 64
HG_SUB = 16
HG_UNROLL = 4
HG_W = HG_HEADS * HG_DK


def _cparams(sem):
    return pltpu.CompilerParams(dimension_semantics=sem, vmem_limit_bytes=VMEM_LIMIT)


def _sigmoid(x):
    return 1.0 / (1.0 + jnp.exp(-x))


def _split_bf16(x):
    hi = x.astype(BF16)
    lo = (x - hi.astype(F32)).astype(BF16)
    return hi, lo


def _dot(a, b):
    return jnp.dot(a, b, preferred_element_type=F32)


def _dot_nt(a, b):
    return lax.dot_general(a, b, (((1,), (1,)), ((), ())), preferred_element_type=F32)


def _dot_tn(a, b):
    return lax.dot_general(a, b, (((0,), (0,)), ((), ())), preferred_element_type=F32)


def _norm_matmul_kernel(x_ref, g_ref, w_ref, o_ref, h_ref):
    @pl.when(pl.program_id(1) == 0)
    def _():
        x = x_ref[...]
        ms = jnp.mean(x * x, axis=-1, keepdims=True)
        h_ref[...] = (x * lax.rsqrt(ms + EPS) * g_ref[...]).astype(BF16)

    o_ref[...] = _dot(h_ref[...], w_ref[...]).astype(o_ref.dtype)


def _layer_spec(a, l):
    nd = a.ndim - 1
    return pl.BlockSpec((None,) + a.shape[1:], lambda *_: (l,) + (0,) * nd)


def _norm_matmul(x2d, g, w, l, tm, tn):
    t, d = x2d.shape
    n = w.shape[2]
    return pl.pallas_call(
        _norm_matmul_kernel,
        grid=(t // tm, n // tn),
        in_specs=[pl.BlockSpec((tm, d), lambda i, j: (i, 0)),
                  _layer_spec(g, l),
                  pl.BlockSpec((None, d, tn), lambda i, j: (l, 0, j))],
        out_specs=pl.BlockSpec((tm, tn), lambda i, j: (i, j)),
        out_shape=jax.ShapeDtypeStruct((t, n), BF16),
        scratch_shapes=[pltpu.VMEM((tm, d), BF16)],
        compiler_params=_cparams(("parallel", "arbitrary")),
        name="norm_matmul",
    )(x2d, g, w)


def _hgrn_kernel(p_ref, lbl_ref, lmask_ref, gout_ref, o_ref, st_ref, *, n_chunks):
    @pl.when(pl.program_id(1) == 0)
    def _():
        st_ref[...] = jnp.zeros_like(st_ref)

    c = HG_CHUNK
    w = HG_W
    logits = lbl_ref[...]
    mx = jnp.max(logits, axis=0, keepdims=True)
    ex = jnp.exp(logits - mx)
    sm = ex / jnp.sum(ex, axis=0, keepdims=True)
    lb = jnp.sum(sm * lmask_ref[...], axis=0, keepdims=True)
    log_lb = jnp.log(lb)
    log_1mlb = jnp.log1p(-lb)
    one_mlb = 1.0 - lb
    gout = gout_ref[...]

    row = lax.broadcasted_iota(jnp.int32, (c, c), 0)
    col = lax.broadcasted_iota(jnp.int32, (c, c), 1)
    tri = (col <= row).astype(BF16)
    rmod = lax.broadcasted_iota(jnp.int32, (c, w), 0) % HG_SUB
    rmod8 = lax.broadcasted_iota(jnp.int32, (c, w), 0) % SUBLANES_F32
    blk_t = lax.broadcasted_iota(jnp.int32, (HG_HEADS * c, c), 0) % c // HG_SUB
    blk_s = lax.broadcasted_iota(jnp.int32, (HG_HEADS * c, c), 1) // HG_SUB
    same_block = blk_t == blk_s
    lane_head = lax.broadcasted_iota(jnp.int32, (c, w), 1) // HG_DK
    eh_r = lax.broadcasted_iota(jnp.int32, (w, w), 0) // HG_DK
    eh_c = lax.broadcasted_iota(jnp.int32, (w, w), 1) // HG_DK
    same_head = eh_r == eh_c
    eh = same_head.astype(BF16)
    n_sub = c // HG_SUB
    cat_head = lax.broadcasted_iota(jnp.int32, (c, (n_sub - 1) * w), 1) % w // HG_DK

    def chunk(ci, carry):
        r0 = pl.multiple_of(ci * c, c)
        blk = p_ref[pl.ds(r0, c), :].astype(F32)
        hq, hf, hv, hg = blk[:, :w], blk[:, w:2 * w], blk[:, 2 * w:3 * w], blk[:, 3 * w:]
        q = hq * _sigmoid(hq)
        e = jnp.exp(-jnp.abs(hf))
        l1pe = jnp.log(1.0 + e)
        logsig = jnp.minimum(hf, 0.0) - l1pe
        b_term = log_1mlb + logsig
        mxab = jnp.maximum(log_lb, b_term)
        lf = mxab + jnp.log(1.0 + jnp.exp(-jnp.abs(log_lb - b_term)))
        kk = one_mlb * jnp.where(hf >= 0.0, e, 1.0) / (1.0 + e)
        v_bf = hv.astype(BF16)

        lf_hi, lf_lo = _split_bf16(lf)
        bcum = _dot(tri, lf_hi) + _dot(tri, lf_lo)
        b_last = bcum[c - 1:c, :]

        st = st_ref[...]
        qe = (q * jnp.exp(bcum)).astype(BF16)
        o = _dot_nt(qe, st.astype(BF16))

        qms, kms = [], []
        for i in range(1, n_sub):
            lo_r, hi_r = i * HG_SUB, (i + 1) * HG_SUB
            ref_row = bcum[lo_r - 1:lo_r, :]
            qi = q[lo_r:hi_r] * jnp.exp(bcum[lo_r:hi_r] - ref_row)
            pieces = [jnp.zeros((lo_r, w), F32), qi]
            if hi_r < c:
                pieces.append(jnp.zeros((c - hi_r, w), F32))
            qms.append(jnp.concatenate(pieces, axis=0))
            ki = kk[:lo_r] * jnp.exp(ref_row - bcum[:lo_r])
            kms.append(jnp.concatenate([ki, jnp.zeros((c - lo_r, w), F32)], axis=0))
        qc = jnp.concatenate(qms, axis=1)
        kc = jnp.concatenate(kms, axis=1).astype(BF16)
        qstack = jnp.concatenate(
            [jnp.where(cat_head == h, qc, 0.0) for h in range(HG_HEADS)], axis=0).astype(BF16)
        a_stack = _dot_nt(qstack, kc)

        half = HG_SUB // 2
        ref2 = jnp.concatenate(
            [jnp.broadcast_to(bcum[i * HG_SUB + half - 1:i * HG_SUB + half, :], (HG_SUB, w))
             for i in range(n_sub)], axis=0)
        upper = rmod >= half
        q2 = jnp.where(upper, q * jnp.exp(jnp.where(upper, bcum - ref2, 0.0)), 0.0)
        k2 = jnp.where(upper, 0.0, kk * jnp.exp(jnp.where(upper, 0.0, ref2 - bcum))).astype(BF16)
        q2stack = jnp.concatenate(
            [jnp.where(lane_head == h, q2, 0.0) for h in range(HG_HEADS)], axis=0).astype(BF16)
        a2 = _dot_nt(q2stack, k2)
        a_tot = a_stack + jnp.where(same_block, a2, 0.0)
        r_full = _dot(a_tot.astype(BF16), v_bf)
        for h in range(HG_HEADS):
            o = o + jnp.where(lane_head == h, r_full[h * c:(h + 1) * c], 0.0)

        def shift(a, d):
            a3 = a.reshape(c // SUBLANES_F32, SUBLANES_F32, w)
            return pltpu.roll(a3, d, 1).reshape(c, w)

        for d in range(half):
            if d == 0:
                ks, bs, vs = kk, bcum, hv
            else:
                ks, bs, vs = shift(kk, d), shift(bcum, d), shift(hv, d)
            valid = rmod8 >= d
            pd = jnp.where(valid, q * ks * jnp.exp(jnp.where(valid, bcum - bs, 0.0)), 0.0)
            o = o + _dot(pd.astype(BF16), eh) * vs

        kdec = (kk * jnp.exp(b_last - bcum)).astype(BF16)
        upd = _dot_tn(v_bf, kdec)
        st_ref[...] = st * jnp.exp(b_last) + jnp.where(same_head, upd, 0.0)

        o2_hi, o2_lo = _split_bf16(o * o)
        ms = (_dot(o2_hi, eh) + _dot(o2_lo, eh)) * (1.0 / HG_DV)
        y = o * lax.rsqrt(ms + EPS) * gout * (hg * _sigmoid(hg))
        o_ref[pl.ds(r0, c), :] = y.astype(o_ref.dtype)
        return carry

    def trip(ti, carry):
        for u in range(HG_UNROLL):
            carry = chunk(ti * HG_UNROLL + u, carry)
        return carry

    lax.fori_loop(0, n_chunks // HG_UNROLL, trip, 0)


def _hgrn(proj, lb_logits, lmasks, gout_t, l, batch, seq, ts):
    t = proj.shape[0]
    nsb = seq // ts
    cb = COL_HG // (4 * HG_W)
    return pl.pallas_call(
        functools.partial(_hgrn_kernel, n_chunks=ts // HG_CHUNK),
        grid=(batch, nsb),
        in_specs=[pl.BlockSpec((ts, 4 * HG_W), lambda b, s: (b * nsb + s, cb)),
                  pl.BlockSpec(lb_logits.shape, lambda b, s: (0, 0)),
                  _layer_spec(lmasks, l),
                  _layer_spec(gout_t, l)],
        out_specs=pl.BlockSpec((ts, HG_W), lambda b, s: (b * nsb + s, 0)),
        out_shape=jax.ShapeDtypeStruct((t, HG_W), BF16),
        scratch_shapes=[pltpu.VMEM((HG_W, HG_W), F32)],
        compiler_params=_cparams(("parallel", "arbitrary")),
        name="hgrn2",
    )(proj, lb_logits, lmasks, gout_t)


def _mla_prep_kernel(p_ref, gq_ref, wuq_ref, gkv_ref, wkc_ref, wuv_ref, gqq_ref, gqk_ref,
                     cos_ref, sin_ref, q_ref, k_ref, v_ref):
    qk_w = MLA_HEADS * HEAD_PAD
    blk = p_ref[...].astype(F32)
    cq = blk[:, :MLA_Q_RANK]
    ckv = blk[:, MLA_Q_RANK:MLA_Q_RANK + MLA_KV_RANK]
    krp = blk[:, MLA_Q_RANK + MLA_KV_RANK:]

    def rms(x, g):
        ms = jnp.mean(x * x, axis=-1, keepdims=True)
        return x * lax.rsqrt(ms + EPS) * g

    cqn = rms(cq, gq_ref[...]).astype(BF16)
    ckvn = rms(ckv, gkv_ref[...]).astype(BF16)
    q_raw = _dot(cqn, wuq_ref[...])
    k_raw = _dot(jnp.concatenate([ckvn, krp.astype(BF16)], axis=1), wkc_ref[...])
    v_ref[...] = _dot(ckvn, wuv_ref[...]).astype(v_ref.dtype)

    cos, sin = cos_ref[...], sin_ref[...]

    def head(raw, h, g_ref, scale):
        x = raw[:, h * HEAD_PAD:(h + 1) * HEAD_PAD]
        xs = raw[:, qk_w + h * HEAD_PAD:qk_w + (h + 1) * HEAD_PAD]
        ms = jnp.sum(x * x, axis=-1, keepdims=True) * (1.0 / MLA_QK)
        r = lax.rsqrt(ms + EPS) * scale
        return (x * g_ref[:, :HEAD_PAD] * cos + xs * g_ref[:, HEAD_PAD:] * sin) * r

    for h in range(MLA_HEADS):
        sl = slice(h * HEAD_PAD, (h + 1) * HEAD_PAD)
        q_ref[:, sl] = head(q_raw, h, gqq_ref, MLA_QK ** -0.5 * LOG2E).astype(q_ref.dtype)
        k_ref[:, sl] = head(k_raw, h, gqk_ref, 1.0).astype(k_ref.dtype)


def _mla_prep(proj, gq, wuq, gkv, wkc, wuv, gqq, gqk, cos_t, sin_t, l, seq, tm):
    t = proj.shape[0]
    nsb = seq // tm
    cb = COL_MLA // MLA_W
    full = lambda a: _layer_spec(a, l)
    tab = pl.BlockSpec((tm, HEAD_PAD), lambda i: (i % nsb, 0))
    qk_w = MLA_HEADS * HEAD_PAD
    return pl.pallas_call(
        _mla_prep_kernel,
        grid=(t // tm,),
        in_specs=[pl.BlockSpec((tm, MLA_W), lambda i: (i, cb)),
                  full(gq), full(wuq), full(gkv), full(wkc), full(wuv), full(gqq), full(gqk),
                  tab, tab],
        out_specs=[pl.BlockSpec((tm, qk_w), lambda i: (i, 0)),
                   pl.BlockSpec((tm, qk_w), lambda i: (i, 0)),
                   pl.BlockSpec((tm, MLA_HEADS * MLA_V), lambda i: (i, 0))],
        out_shape=[jax.ShapeDtypeStruct((t, qk_w), BF16),
                   jax.ShapeDtypeStruct((t, qk_w), BF16),
                   jax.ShapeDtypeStruct((t, MLA_HEADS * MLA_V), BF16)],
        compiler_params=_cparams(("parallel",)),
        name="mla_prep",
    )(proj, gq, wuq, gkv, wkc, wuv, gqq, gqk, cos_t, sin_t)


def _mla_attn_kernel(q_ref, k_ref, v_ref, o_ref, *, tq, tk):
    i = pl.program_id(2)
    n_diag = tq // tk
    lane = lax.broadcasted_iota(jnp.int32, (tq, LANES), 1)
    row = lax.broadcasted_iota(jnp.int32, (tq, tk), 0)
    col = lax.broadcasted_iota(jnp.int32, (tq, tk), 1)
    qs = [q_ref[:, hh * HEAD_PAD:(hh + 1) * HEAD_PAD] for hh in range(2)]

    def step(kc, carry, dj):
        k0 = pl.multiple_of(kc * tk, tk)
        vb = v_ref[pl.ds(k0, tk), :]
        out = []
        for hh in range(2):
            m, l, acc = carry[hh]
            kb = k_ref[pl.ds(k0, tk), hh * HEAD_PAD:(hh + 1) * HEAD_PAD]
            s = _dot_nt(qs[hh], kb)
            if dj is not None:
                s = jnp.where(col + dj * tk <= row, s, -jnp.inf)
            m_new = jnp.maximum(m, jnp.max(s, axis=-1, keepdims=True))
            alpha = jnp.exp2(m - m_new)
            p = jnp.exp2(s - m_new)
            l = alpha * l + jnp.sum(p, axis=-1, keepdims=True)
            acc = alpha * acc + _dot(p.astype(BF16), vb)
            out.append((m_new, l, acc))
        return tuple(out)

    one = (jnp.full((tq, 1), -jnp.inf, F32), jnp.zeros((tq, 1), F32), jnp.zeros((tq, LANES), F32))
    carry = step(i * n_diag, (one, one), 0)
    for dj in range(1, n_diag):
        carry = step(i * n_diag + dj, carry, dj)
    carry = lax.fori_loop(0, i * n_diag, lambda kc, cr: step(kc, cr, None), carry)
    outs = [acc / l for (_, l, acc) in carry]
    o_ref[...] = jnp.where(lane < MLA_V, outs[0], outs[1]).astype(o_ref.dtype)


def _mla_attn(q, k, v, batch, seq, tq, tk):
    t = q.shape[0]
    nqb = seq // tq
    return pl.pallas_call(
        functools.partial(_mla_attn_kernel, tq=tq, tk=tk),
        grid=(batch, MLA_HEADS // 2, nqb),
        in_specs=[pl.BlockSpec((tq, 2 * HEAD_PAD), lambda b, p, i: (b * nqb + i, p)),
                  pl.BlockSpec((seq, 2 * HEAD_PAD), lambda b, p, i: (b, p)),
                  pl.BlockSpec((seq, LANES), lambda b, p, i: (b, p))],
        out_specs=pl.BlockSpec((tq, LANES), lambda b, p, i: (b * nqb + i, p)),
        out_shape=jax.ShapeDtypeStruct((t, MLA_HEADS * MLA_V), BF16),
        compiler_params=_cparams(("parallel", "parallel", "arbitrary")),
        name="mla_attn",
    )(q, k, v)


CONV_HALO = 32
CONV_ROWS = 64


SUBLANES = SUBLANES_F32


def _conv_kernel(p_ref, w_ref, b_ref, g_ref, beta_ref, o_ref, halo_ref, v_ref, *, ts):
    @pl.when(pl.program_id(1) == 0)
    def _():
        halo_ref[...] = jnp.zeros_like(halo_ref)

    blk = p_ref[...].astype(F32)
    u = blk[:, :CONV_CH] * _sigmoid(blk[:, CONV_CH:])
    u_ext = jnp.concatenate([halo_ref[...], u], axis=0)
    halo_ref[...] = u[ts - CONV_HALO:]
    n_ext = CONV_HALO + ts
    v_ref[0] = u_ext
    for m in range(1, SUBLANES):
        v_ref[m] = pltpu.roll(u_ext, n_ext - m, 0)
    w = w_ref[...]
    bias = b_ref[...]
    gam, beta = g_ref[...], beta_ref[...]
    first = CONV_HALO - (CONV_K - 1)
    for r in range(ts // CONV_ROWS):
        acc = jnp.zeros((CONV_ROWS, CONV_CH), F32) + bias
        for j in range(CONV_K):
            off = first + j
            start = r * CONV_ROWS + (off // SUBLANES) * SUBLANES
            acc = acc + w[j:j + 1, :] * v_ref[off % SUBLANES, start:start + CONV_ROWS, :]
        mu = jnp.mean(acc, axis=-1, keepdims=True)
        xc = acc - mu
        var = jnp.mean(xc * xc, axis=-1, keepdims=True)
        y = xc * lax.rsqrt(var + LN_EPS) * gam + beta
        o_ref[r * CONV_ROWS:(r + 1) * CONV_ROWS, :] = (y * _sigmoid(y)).astype(o_ref.dtype)


def _conformer_conv(proj, w, b, g, beta, l, batch, seq, ts):
    t = proj.shape[0]
    nsb = seq // ts
    cb = COL_CU // (2 * CONV_CH)
    full = lambda a: _layer_spec(a, l)
    return pl.pallas_call(
        functools.partial(_conv_kernel, ts=ts),
        grid=(batch, nsb),
        in_specs=[pl.BlockSpec((ts, 2 * CONV_CH), lambda bb, s: (bb * nsb + s, cb)),
                  full(w), full(b), full(g), full(beta)],
        out_specs=pl.BlockSpec((ts, CONV_CH), lambda bb, s: (bb * nsb + s, 0)),
        out_shape=jax.ShapeDtypeStruct((t, CONV_CH), BF16),
        scratch_shapes=[pltpu.VMEM((CONV_HALO, CONV_CH), F32),
                        pltpu.VMEM((SUBLANES, CONV_HALO + ts, CONV_CH), F32)],
        compiler_params=_cparams(("parallel", "arbitrary")),
        name="conformer_conv",
    )(proj, w, b, g, beta)


SB_SUB = 256


def _sb_attn_kernel(q_ref, k_ref, v_ref, o_ref, *, tq, tk):
    i = pl.program_id(2)
    n_diag = tq // tk
    n_sub = tk // SB_SUB
    lane = lax.broadcasted_iota(jnp.int32, (tq, LANES), 1)
    row = lax.broadcasted_iota(jnp.int32, (tq, SB_SUB), 0)
    col = lax.broadcasted_iota(jnp.int32, (tq, SB_SUB), 1)
    r2 = lax.broadcasted_iota(jnp.int32, (SB_SUB, SB_SUB), 0)
    c2 = lax.broadcasted_iota(jnp.int32, (SB_SUB, SB_SUB), 1)
    suffix = (r2 >= c2).astype(BF16)
    q = q_ref[...]
    qs = [jnp.where((lane >= hh * SB_HEAD_DIM) & (lane < (hh + 1) * SB_HEAD_DIM), q,
                    jnp.zeros_like(q)) for hh in range(2)]

    def step(kc, carry, dj):
        k0 = pl.multiple_of(kc * tk, tk)
        kb = k_ref[pl.ds(k0, tk), :]
        vb = v_ref[pl.ds(k0, tk), :]
        out = []
        for hh in range(2):
            run, acc = carry[hh]
            z_all = _dot_nt(qs[hh], kb)
            wgts = [None] * n_sub
            for sub in reversed(range(n_sub)):
                z = z_all[:, sub * SB_SUB:(sub + 1) * SB_SUB]
                sp = jnp.maximum(z, 0.0) + jnp.log(1.0 + jnp.exp2(-jnp.abs(z))) * LOG2E
                if dj is not None:
                    valid = col + (dj * tk + sub * SB_SUB) < row
                    sp = jnp.where(valid, sp, 0.0)
                hi, lo = _split_bf16(sp)
                incl = _dot(hi, suffix) + _dot(lo, suffix)
                wgt = jnp.exp2(z - incl - run)
                if dj is not None:
                    wgt = jnp.where(valid, wgt, 0.0)
                wgts[sub] = wgt.astype(BF16)
                run = run + incl[:, :1]
            wcat = wgts[0] if n_sub == 1 else jnp.concatenate(wgts, axis=1)
            out.append((run, acc + _dot(wcat, vb)))
        return tuple(out)

    one = (jnp.zeros((tq, 1), F32), jnp.zeros((tq, LANES), F32))
    carry = (one, one)
    for dj in reversed(range(n_diag)):
        carry = step(i * n_diag + dj, carry, dj)
    nfull = i * n_diag
    carry = lax.fori_loop(0, nfull, lambda it, cr: step(nfull - 1 - it, cr, None), carry)
    o_ref[...] = jnp.where(lane < SB_HEAD_DIM, carry[0][1], carry[1][1]).astype(o_ref.dtype)


def _sb_attn(proj, batch, seq, tq, tk):
    t = proj.shape[0]
    nqb = seq // tq
    qc, kc, vc = (COL_SB // LANES, (COL_SB + 256) // LANES, (COL_SB + 512) // LANES)
    return pl.pallas_call(
        functools.partial(_sb_attn_kernel, tq=tq, tk=tk),
        grid=(batch, SB_HEADS // 2, nqb),
        in_specs=[pl.BlockSpec((tq, LANES), lambda b, p, i: (b * nqb + i, qc + p)),
                  pl.BlockSpec((seq, LANES), lambda b, p, i: (b, kc + p)),
                  pl.BlockSpec((seq, LANES), lambda b, p, i: (b, vc + p))],
        out_specs=pl.BlockSpec((tq, LANES), lambda b, p, i: (b * nqb + i, p)),
        out_shape=jax.ShapeDtypeStruct((t, SB_HEADS * SB_HEAD_DIM), BF16),
        compiler_params=_cparams(("parallel", "parallel", "arbitrary")),
        name="sb_attn",
    )(proj, proj, proj)


def _merge_kernel(x_ref, gl_ref, ya_ref, yb_ref, yc_ref, yd_ref, wb_ref, wo_ref, o_ref):
    merged = None
    for n, y_ref in enumerate((ya_ref, yb_ref, yc_ref, yd_ref)):
        hb = _dot(y_ref[...], wb_ref[n])
        th = jnp.tanh(gl_ref[:, n * D_MODEL:(n + 1) * D_MODEL].astype(F32))
        term = hb + hb * th
        merged = term if merged is None else merged + term
    o_ref[...] = x_ref[...] + _dot(merged.astype(BF16), wo_ref[...])


def _merge(x2d, proj, ya, yb, yc, yd, wb, wo, l, tm):
    t, d = x2d.shape
    yspec = pl.BlockSpec((tm, BRANCH_W), lambda i: (i, 0))
    return pl.pallas_call(
        _merge_kernel,
        grid=(t // tm,),
        in_specs=[pl.BlockSpec((tm, d), lambda i: (i, 0)),
                  pl.BlockSpec((tm, N_BRANCH * d), lambda i: (i, COL_GATE // (N_BRANCH * d))),
                  yspec, yspec, yspec, yspec,
                  _layer_spec(wb, l), _layer_spec(wo, l)],
        out_specs=pl.BlockSpec((tm, d), lambda i: (i, 0)),
        out_shape=jax.ShapeDtypeStruct((t, d), F32),
        compiler_params=_cparams(("parallel",)),
        name="merge_out",
    )(x2d, proj, ya, yb, yc, yd, wb, wo)


FFN_HALO = 16
FFN_ROW_SPLIT = 8


def _ffn_kernel(x_ref, xh_ref, g_ref, wv_ref, wg_ref, cwv_ref, cwg_ref, cbv_ref, cbg_ref,
                wd_ref, o_ref, h_ref, acc_ref, act_a, act_b, *, nk, n_row_blocks, blocks_per_seq):
    s = pl.program_id(0)
    c = s % nk
    i = jnp.minimum(s // nk, n_row_blocks - 1)

    def norm(x):
        ms = jnp.mean(x * x, axis=-1, keepdims=True)
        return (x * lax.rsqrt(ms + EPS) * g_ref[...]).astype(BF16)

    @pl.when(s == 0)
    def _():
        act_a[...] = jnp.zeros_like(act_a)
        acc_ref[...] = jnp.zeros_like(acc_ref)

    @pl.when(c == 0)
    def _():
        h_ref[FFN_HALO:, :] = norm(x_ref[...])
        keep = (i % blocks_per_seq != 0).astype(F32)
        h_ref[0:FFN_HALO, :] = norm(xh_ref[...] * keep)

    def conv(ext, cw_ref, cb_ref):
        cw = cw_ref[...]
        u1 = pltpu.roll(ext, 1, 0)[FFN_HALO:]
        u2 = pltpu.roll(ext, 2, 0)[FFN_HALO:]
        return cw[2:3] * ext[FFN_HALO:] + cw[1:2] * u1 + cw[0:1] * u2 + cb_ref[...]

    tm = acc_ref.shape[0]
    rs = tm // FFN_ROW_SPLIT

    def main(act_in, act_out):
        tails = None
        for j in range(FFN_ROW_SPLIT):
            lo = FFN_HALO + j * rs
            if j == 0:
                hs = h_ref[0:lo + rs, :]
                exts = [_dot(hs, w_ref[...]) for w_ref in (wv_ref, wg_ref)]
            else:
                hs = h_ref[lo:lo + rs, :]
                exts = [jnp.concatenate([t, _dot(hs, w_ref[...])], axis=0)
                        for t, w_ref in zip(tails, (wv_ref, wg_ref))]
            tails = [e[rs:] for e in exts]
            val = conv(exts[0], cwv_ref, cbv_ref)
            hg = conv(exts[1], cwg_ref, cbg_ref)
            act_out[j * rs:(j + 1) * rs, :] = ((hg + hg * jnp.tanh(hg)) * val).astype(BF16)
        acc_ref[...] += _dot(act_in[...], wd_ref[...])

    @pl.when(s % 2 == 0)
    def _():
        main(act_a, act_b)

    @pl.when(s % 2 == 1)
    def _():
        main(act_b, act_a)

    @pl.when(c == 0)
    def _():
        o_ref[...] = acc_ref[...]
        acc_ref[...] = x_ref[...]


def _ffn(x2d, g, w_up, cw, cb, w_down, l, seq, tm, tk):
    t, d = x2d.shape
    nk = D_FF // tk
    hb = tm // FFN_HALO
    nrb = t // tm
    row_blk = lambda s: jnp.minimum(s // nk, nrb - 1)
    return pl.pallas_call(
        functools.partial(_ffn_kernel, nk=nk, n_row_blocks=nrb, blocks_per_seq=seq // tm),
        grid=(nrb * nk + 1,),
        in_specs=[pl.BlockSpec((tm, d), lambda s: (row_blk(s), 0)),
                  pl.BlockSpec((FFN_HALO, d), lambda s: (jnp.maximum(row_blk(s) * hb - 1, 0), 0)),
                  _layer_spec(g, l),
                  pl.BlockSpec((None, d, tk), lambda s: (l, 0, s % nk)),
                  pl.BlockSpec((None, d, tk), lambda s: (l, 0, nk + s % nk)),
                  pl.BlockSpec((None, FFN_CONV_K, tk), lambda s: (l, 0, s % nk)),
                  pl.BlockSpec((None, FFN_CONV_K, tk), lambda s: (l, 0, nk + s % nk)),
                  pl.BlockSpec((None, 1, tk), lambda s: (l, 0, s % nk)),
                  pl.BlockSpec((None, 1, tk), lambda s: (l, 0, nk + s % nk)),
                  pl.BlockSpec((None, tk, d), lambda s: (l, (s + nk - 1) % nk, 0))],
        out_specs=pl.BlockSpec((tm, d), lambda s: (jnp.maximum(s - 1, 0) // nk, 0)),
        out_shape=jax.ShapeDtypeStruct((t, d), F32),
        scratch_shapes=[pltpu.VMEM((tm + FFN_HALO, d), BF16), pltpu.VMEM((tm, d), F32),
                        pltpu.VMEM((tm, tk), BF16), pltpu.VMEM((tm, tk), BF16)],
        compiler_params=_cparams(("arbitrary",)),
        name="ffn",
    )(x2d, x2d, g, w_up, w_up, cw, cw, cb, cb, w_down)


def _prep_w_in(w_in):
    offs = np.concatenate([[0], np.cumsum(IN_COLS)])
    seg = lambda i, c=1.0: (w_in[..., offs[i]:offs[i + 1]] * c).astype(BF16)
    pad = jnp.zeros(w_in.shape[:-1] + (MLA_W - MLA_Q_RANK - MLA_KV_RANK - MLA_ROPE,), BF16)
    parts = [seg(11, 0.5), seg(0), seg(1), seg(2), seg(3), seg(4), seg(5), seg(6), pad,
             seg(7), seg(8, SB_HEAD_DIM ** -0.5 * LOG2E), seg(9), seg(10)]
    return jnp.concatenate(parts, axis=-1)


def _prep_mla_weights(w_uq, w_ukv):
    nl = w_uq.shape[0]
    wq = w_uq.reshape(nl, MLA_Q_RANK, MLA_HEADS, MLA_QK)
    wq = jnp.pad(wq, ((0, 0), (0, 0), (0, 0), (0, HEAD_PAD - MLA_QK)))
    wq = wq.reshape(nl, MLA_Q_RANK, MLA_HEADS * HEAD_PAD)
    wkv = w_ukv.reshape(nl, MLA_KV_RANK, MLA_HEADS, MLA_NOPE + MLA_V)
    wk = jnp.pad(wkv[..., :MLA_NOPE], ((0, 0), (0, 0), (0, 0), (0, HEAD_PAD - MLA_NOPE)))
    wk = wk.reshape(nl, MLA_KV_RANK, MLA_HEADS * HEAD_PAD)
    wv = wkv[..., MLA_NOPE:].reshape(nl, MLA_KV_RANK, MLA_HEADS * MLA_V)
    place = np.zeros((LANES, MLA_HEADS * HEAD_PAD), np.float32)
    for h in range(MLA_HEADS):
        for r in range(MLA_ROPE):
            place[r, h * HEAD_PAD + MLA_NOPE + r] = 1.0
    wkc = jnp.concatenate([wk, jnp.broadcast_to(jnp.asarray(place), (nl,) + place.shape)], axis=1)
    wq = jnp.concatenate([wq, _swap_rope_lanes(wq)], axis=-1)
    wkc = jnp.concatenate([wkc, _swap_rope_lanes(wkc)], axis=-1)
    return wq.astype(BF16), wkc.astype(BF16), wv.astype(BF16)


def _swap_rope_lanes(a):
    half = MLA_ROPE // 2
    perm = np.arange(HEAD_PAD)
    perm[MLA_NOPE:MLA_NOPE + half] = np.arange(MLA_NOPE + half, MLA_QK)
    perm[MLA_NOPE + half:MLA_QK] = np.arange(MLA_NOPE, MLA_NOPE + half)
    groups = a.shape[-1] // HEAD_PAD
    idx = (np.arange(groups)[:, None] * HEAD_PAD + perm[None, :]).reshape(-1)
    return a[..., idx]


def _qk_gain(g):
    gp = jnp.pad(g, ((0, 0), (0, HEAD_PAD - MLA_QK)))
    return jnp.concatenate([gp, _swap_rope_lanes(gp)], axis=-1)[:, None, :]


def _rope_tables(seq):
    half = MLA_ROPE // 2
    freqs = ROPE_BASE ** (-jnp.arange(half, dtype=F32) / half)
    ang = jnp.arange(seq, dtype=F32)[:, None] * freqs[None, :]
    cos, sin = jnp.cos(ang), jnp.sin(ang)
    tail = HEAD_PAD - MLA_QK
    cos_t = jnp.concatenate([jnp.ones((seq, MLA_NOPE), F32), cos, cos, jnp.ones((seq, tail), F32)],
                            axis=1)
    sin_t = jnp.concatenate([jnp.zeros((seq, MLA_NOPE), F32), -sin, sin, jnp.zeros((seq, tail), F32)],
                            axis=1)
    return cos_t, sin_t


def kernel(x, g_mix, w_in, lb_logits, g_hg_out, g_q_lat, w_uq, g_kv_lat, w_ukv, g_qk_q, g_qk_k,
           conv_w, conv_b, conv_ln_g, conv_ln_b, w_branch, w_out, g_ffn, w_up, ffn_conv_w,
           ffn_conv_b, w_down):
    batch, seq, d = x.shape
    depth = w_in.shape[0]
    t = batch * seq

    tm_proj = min(1024, seq)
    ts_hg = min(512, seq)
    tm_mla = min(512, seq)
    tq_attn = min(1024, seq)
    tk_attn = 512
    ts_conv = min(512, seq)
    tm_merge = min(512, seq)
    tm_ffn = min(1024, seq)

    w_in_p = _prep_w_in(w_in)
    wq_p, wkc_p, wv_p = _prep_mla_weights(w_uq, w_ukv)
    cos_t, sin_t = _rope_tables(seq)
    rows = lambda a: a[:, None, :]
    gqq_p = _qk_gain(g_qk_q)
    gqk_p = _qk_gain(g_qk_k)
    gout_t = rows(jnp.tile(g_hg_out, (1, HG_HEADS)))
    w_branch_b = (0.5 * w_branch).astype(BF16)
    w_out_b = w_out.astype(BF16)
    w_up_b = w_up.astype(BF16)
    w_down_b = w_down.astype(BF16)
    ids = np.arange(depth)
    lmasks = jnp.asarray(((ids[None, :] >= 1) & (ids[None, :] <= ids[:, None]))
                         .astype(np.float32)[:, :, None])
    lb_logits = lb_logits.astype(F32)
    g_mix3, g_ffn3, g_q3, g_kv3 = rows(g_mix), rows(g_ffn), rows(g_q_lat), rows(g_kv_lat)
    conv_b3, ln_g3, ln_b3 = rows(conv_b), rows(conv_ln_g), rows(conv_ln_b)
    gate_half = jnp.asarray(np.concatenate([np.ones(D_FF, np.float32), np.full(D_FF, 0.5, np.float32)]))
    ffn_cw = ffn_conv_w * gate_half
    ffn_b3 = rows(ffn_conv_b * gate_half)

    x2d = x.reshape(t, d)
    for l in range(depth):
        proj = _norm_matmul(x2d, g_mix3, w_in_p, l, tm_proj, 768)
        y_a = _hgrn(proj, lb_logits, lmasks, gout_t, l, batch, seq, ts_hg)
        q, k, v = _mla_prep(proj, g_q3, wq_p, g_kv3, wkc_p, wv_p, gqq_p, gqk_p,
                            cos_t, sin_t, l, seq, tm_mla)
        y_b = _mla_attn(q, k, v, batch, seq, tq_attn, tk_attn)
        y_c = _conformer_conv(proj, conv_w, conv_b3, ln_g3, ln_b3, l, batch, seq, ts_conv)
        y_d = _sb_attn(proj, batch, seq, tq_attn, tk_attn)
        x2d = _merge(x2d, proj, y_a, y_b, y_c, y_d, w_branch_b, w_out_b, l, tm_merge)
        x2d = _ffn(x2d, g_ffn3, w_up_b, ffn_cw, ffn_b3, w_down_b, l, seq, tm_ffn, 256)
    return x2d.reshape(batch, seq, d)
```

```python
import functools

import jax
import jax.numpy as jnp
import numpy as np
from jax import lax
from jax.experimental import pallas as pl
from jax.experimental.pallas import tpu as pltpu

F32 = jnp.float32
BF16 = jnp.bfloat16

D_MODEL = 1024
HG_HEADS = 4
HG_DK = 64
HG_DV = 64
MLA_HEADS = 4
MLA_Q_RANK = 256
MLA_KV_RANK = 128
MLA_NOPE = 64
MLA_ROPE = 32
MLA_V = 64
MLA_QK = MLA_NOPE + MLA_ROPE
ROPE_BASE = 10000.0
CONV_CH = 256
CONV_K = 31
SB_HEADS = 4
SB_HEAD_DIM = 64
N_BRANCH = 4
BRANCH_W = 256
D_FF = 2816
FFN_CONV_K = 3
EPS = 1e-6
LN_EPS = 1e-5
LOG2E = 1.4426950408889634

IN_COLS = (256, 256, 256, 256, MLA_Q_RANK, MLA_KV_RANK, MLA_ROPE, 2 * CONV_CH, 256, 256, 256,
           N_BRANCH * D_MODEL)

LANES = 128
BF16_ROWS = 16
SUBLANES_F32 = 8
VMEM_LIMIT = 56 * 1024 * 1024

COL_GATE = 0
COL_HG = 4096
COL_MLA = 5120
COL_CU = 5632
COL_SB = 6144
N_IN = 6912
MLA_W = 512
TN_PROJ = 768
TK_FFN = 256
HEAD_PAD = 128

HG_CHUNK = 64
HG_SUB = 16
HG_UNROLL = 4
HG_W = HG_HEADS * HG_DK


def _cparams(sem):
    return pltpu.CompilerParams(dimension_semantics=sem, vmem_limit_bytes=VMEM_LIMIT)


def _sigmoid(x):
    return 1.0 / (1.0 + jnp.exp(-x))


def _neg_abs(x):
    bits = pltpu.bitcast(x, jnp.uint32) | jnp.uint32(0x80000000)
    return pltpu.bitcast(bits, F32)


def _split_bf16(x):
    hi = x.astype(BF16)
    lo = (x - hi.astype(F32)).astype(BF16)
    return hi, lo


def _dot(a, b):
    return jnp.dot(a, b, preferred_element_type=F32)


def _dot_nt(a, b):
    return lax.dot_general(a, b, (((1,), (1,)), ((), ())), preferred_element_type=F32)


def _dot_tn(a, b):
    return lax.dot_general(a, b, (((0,), (0,)), ((), ())), preferred_element_type=F32)


def _norm_matmul_kernel(x_ref, g_ref, w_ref, o_ref, h_ref):
    @pl.when(pl.program_id(1) == 0)
    def _():
        x = x_ref[...]
        ms = jnp.mean(x * x, axis=-1, keepdims=True)
        h_ref[...] = (x * lax.rsqrt(ms + EPS) * g_ref[...]).astype(BF16)

    o_ref[...] = _dot(h_ref[...], w_ref[...]).astype(o_ref.dtype)


def _layer_spec(a, l):
    nd = a.ndim - 1
    return pl.BlockSpec((None,) + a.shape[1:], lambda *_: (l,) + (0,) * nd)


def _col_chunks(w, tn):
    nl, d, n = w.shape
    return w.reshape(nl, d, n // tn, tn).transpose(0, 2, 1, 3)


def _norm_matmul(x2d, g, w, l, tm):
    t, d = x2d.shape
    nj, tn = w.shape[1], w.shape[3]
    n = nj * tn
    return pl.pallas_call(
        _norm_matmul_kernel,
        grid=(t // tm, nj),
        in_specs=[pl.BlockSpec((tm, d), lambda i, j: (i, 0)),
                  _layer_spec(g, l),
                  pl.BlockSpec((None, None, d, tn), lambda i, j: (l, j, 0, 0))],
        out_specs=pl.BlockSpec((tm, tn), lambda i, j: (i, j)),
        out_shape=jax.ShapeDtypeStruct((t, n), BF16),
        scratch_shapes=[pltpu.VMEM((tm, d), BF16)],
        compiler_params=_cparams(("parallel", "arbitrary")),
        name="norm_matmul",
    )(x2d, g, w)


def _hgrn_kernel(p_ref, lbl_ref, lmask_ref, gout_ref, o_ref, st_ref, *, n_chunks):
    @pl.when(pl.program_id(1) == 0)
    def _():
        st_ref[...] = jnp.zeros_like(st_ref)

    c = HG_CHUNK
    w = HG_W
    logits = lbl_ref[...]
    mx = jnp.max(logits, axis=0, keepdims=True)
    ex = jnp.exp(logits - mx)
    sm = ex / jnp.sum(ex, axis=0, keepdims=True)
    lb = jnp.sum(sm * lmask_ref[...], axis=0, keepdims=True)
    log_lb = jnp.log(lb)
    log_1mlb = jnp.log1p(-lb)
    one_mlb = 1.0 - lb
    gout = gout_ref[...]

    row = lax.broadcasted_iota(jnp.int32, (c, c), 0)
    col = lax.broadcasted_iota(jnp.int32, (c, c), 1)
    tri = (col <= row).astype(BF16)
    rmod = lax.broadcasted_iota(jnp.int32, (c, w), 0) % HG_SUB
    rmod8 = lax.broadcasted_iota(jnp.int32, (c, w), 0) % SUBLANES_F32
    blk_t = lax.broadcasted_iota(jnp.int32, (HG_HEADS * c, c), 0) % c // HG_SUB
    blk_s = lax.broadcasted_iota(jnp.int32, (HG_HEADS * c, c), 1) // HG_SUB
    same_block = blk_t == blk_s
    lane_head = lax.broadcasted_iota(jnp.int32, (c, w), 1) // HG_DK
    eh_r = lax.broadcasted_iota(jnp.int32, (w, w), 0) // HG_DK
    eh_c = lax.broadcasted_iota(jnp.int32, (w, w), 1) // HG_DK
    same_head = eh_r == eh_c
    eh = same_head.astype(BF16)
    n_sub = c // HG_SUB
    cat_head = lax.broadcasted_iota(jnp.int32, (c, (n_sub - 1) * w), 1) % w // HG_DK

    def chunk(ci, carry):
        r0 = pl.multiple_of(ci * c, c)
        blk = p_ref[pl.ds(r0, c), :].astype(F32)
        hq, hf, hv, hg = blk[:, :w], blk[:, w:2 * w], blk[:, 2 * w:3 * w], blk[:, 3 * w:]
        q = hq * _sigmoid(hq)
        e = jnp.exp(-jnp.abs(hf))
        l1pe = jnp.log(1.0 + e)
        logsig = jnp.minimum(hf, 0.0) - l1pe
        b_term = log_1mlb + logsig
        mxab = jnp.maximum(log_lb, b_term)
        lf = mxab + jnp.log(1.0 + jnp.exp(-jnp.abs(log_lb - b_term)))
        kk = one_mlb * jnp.where(hf >= 0.0, e, 1.0) / (1.0 + e)
        v_bf = hv.astype(BF16)

        lf_hi, lf_lo = _split_bf16(lf)
        bcum = _dot(tri, lf_hi) + _dot(tri, lf_lo)
        b_last = bcum[c - 1:c, :]

        st = st_ref[...]
        qe = (q * jnp.exp(bcum)).astype(BF16)
        o = _dot_nt(qe, st.astype(BF16))

        qms, kms = [], []
        for i in range(1, n_sub):
            lo_r, hi_r = i * HG_SUB, (i + 1) * HG_SUB
            ref_row = bcum[lo_r - 1:lo_r, :]
            qi = q[lo_r:hi_r] * jnp.exp(bcum[lo_r:hi_r] - ref_row)
            pieces = [jnp.zeros((lo_r, w), F32), qi]
            if hi_r < c:
                pieces.append(jnp.zeros((c - hi_r, w), F32))
            qms.append(jnp.concatenate(pieces, axis=0))
            ki = kk[:lo_r] * jnp.exp(ref_row - bcum[:lo_r])
            kms.append(jnp.concatenate([ki, jnp.zeros((c - lo_r, w), F32)], axis=0))
        qc = jnp.concatenate(qms, axis=1)
        kc = jnp.concatenate(kms, axis=1).astype(BF16)
        qstack = jnp.concatenate(
            [jnp.where(cat_head == h, qc, 0.0) for h in range(HG_HEADS)], axis=0).astype(BF16)
        a_stack = _dot_nt(qstack, kc)

        half = HG_SUB // 2
        ref2 = jnp.concatenate(
            [jnp.broadcast_to(bcum[i * HG_SUB + half - 1:i * HG_SUB + half, :], (HG_SUB, w))
             for i in range(n_sub)], axis=0)
        upper = rmod >= half
        q2 = jnp.where(upper, q * jnp.exp(jnp.where(upper, bcum - ref2, 0.0)), 0.0)
        k2 = jnp.where(upper, 0.0, kk * jnp.exp(jnp.where(upper, 0.0, ref2 - bcum))).astype(BF16)
        q2stack = jnp.concatenate(
            [jnp.where(lane_head == h, q2, 0.0) for h in range(HG_HEADS)], axis=0).astype(BF16)
        a2 = _dot_nt(q2stack, k2)
        a_tot = a_stack + jnp.where(same_block, a2, 0.0)
        r_full = _dot(a_tot.astype(BF16), v_bf)
        for h in range(HG_HEADS):
            o = o + jnp.where(lane_head == h, r_full[h * c:(h + 1) * c], 0.0)

        def shift(a, d):
            a3 = a.reshape(c // SUBLANES_F32, SUBLANES_F32, w)
            return pltpu.roll(a3, d, 1).reshape(c, w)

        for d in range(half):
            if d == 0:
                ks, bs, vs = kk, bcum, hv
            else:
                ks, bs, vs = shift(kk, d), shift(bcum, d), shift(hv, d)
            valid = rmod8 >= d
            pd = jnp.where(valid, q * ks * jnp.exp(jnp.where(valid, bcum - bs, 0.0)), 0.0)
            o = o + _dot(pd.astype(BF16), eh) * vs

        kdec = (kk * jnp.exp(b_last - bcum)).astype(BF16)
        upd = _dot_tn(v_bf, kdec)
        st_ref[...] = st * jnp.exp(b_last) + jnp.where(same_head, upd, 0.0)

        o2_hi, o2_lo = _split_bf16(o * o)
        ms = (_dot(o2_hi, eh) + _dot(o2_lo, eh)) * (1.0 / HG_DV)
        y = o * lax.rsqrt(ms + EPS) * gout * (hg * _sigmoid(hg))
        o_ref[pl.ds(r0, c), :] = y.astype(o_ref.dtype)
        return carry

    def trip(ti, carry):
        for u in range(HG_UNROLL):
            carry = chunk(ti * HG_UNROLL + u, carry)
        return carry

    lax.fori_loop(0, n_chunks // HG_UNROLL, trip, 0)


def _hgrn(proj, lb_logits, lmasks, gout_t, l, batch, seq, ts):
    t = proj.shape[0]
    nsb = seq // ts
    cb = COL_HG // (4 * HG_W)
    return pl.pallas_call(
        functools.partial(_hgrn_kernel, n_chunks=ts // HG_CHUNK),
        grid=(batch, nsb),
        in_specs=[pl.BlockSpec((ts, 4 * HG_W), lambda b, s: (b * nsb + s, cb)),
                  pl.BlockSpec(lb_logits.shape, lambda b, s: (0, 0)),
                  _layer_spec(lmasks, l),
                  _layer_spec(gout_t, l)],
        out_specs=pl.BlockSpec((ts, HG_W), lambda b, s: (b * nsb + s, 0)),
        out_shape=jax.ShapeDtypeStruct((t, HG_W), BF16),
        scratch_shapes=[pltpu.VMEM((HG_W, HG_W), F32)],
        compiler_params=_cparams(("parallel", "arbitrary")),
        name="hgrn2",
    )(proj, lb_logits, lmasks, gout_t)


def _mla_prep_kernel(p_ref, gq_ref, wuq_ref, gkv_ref, wkc_ref, wuv_ref, gqq_ref, gqk_ref,
                     cos_ref, sin_ref, q_ref, k_ref, v_ref):
    qk_w = MLA_HEADS * HEAD_PAD
    blk = p_ref[...].astype(F32)
    cq = blk[:, :MLA_Q_RANK]
    ckv = blk[:, MLA_Q_RANK:MLA_Q_RANK + MLA_KV_RANK]
    krp = blk[:, MLA_Q_RANK + MLA_KV_RANK:]

    def rms(x, g):
        ms = jnp.mean(x * x, axis=-1, keepdims=True)
        return x * lax.rsqrt(ms + EPS) * g

    cqn = rms(cq, gq_ref[...]).astype(BF16)
    ckvn = rms(ckv, gkv_ref[...]).astype(BF16)
    q_raw = _dot(cqn, wuq_ref[...])
    k_raw = _dot(jnp.concatenate([ckvn, krp.astype(BF16)], axis=1), wkc_ref[...])
    v_ref[...] = _dot(ckvn, wuv_ref[...]).astype(v_ref.dtype)

    cos, sin = cos_ref[...], sin_ref[...]

    def head(raw, h, g_ref, scale):
        x = raw[:, h * HEAD_PAD:(h + 1) * HEAD_PAD]
        xs = raw[:, qk_w + h * HEAD_PAD:qk_w + (h + 1) * HEAD_PAD]
        ms = jnp.sum(x * x, axis=-1, keepdims=True) * (1.0 / MLA_QK)
        r = lax.rsqrt(ms + EPS) * scale
        return (x * g_ref[:, :HEAD_PAD] * cos + xs * g_ref[:, HEAD_PAD:] * sin) * r

    for h in range(MLA_HEADS):
        sl = slice(h * HEAD_PAD, (h + 1) * HEAD_PAD)
        q_ref[:, sl] = head(q_raw, h, gqq_ref, MLA_QK ** -0.5 * LOG2E).astype(q_ref.dtype)
        k_ref[:, sl] = head(k_raw, h, gqk_ref, 1.0).astype(k_ref.dtype)


def _mla_prep(proj, gq, wuq, gkv, wkc, wuv, gqq, gqk, cos_t, sin_t, l, seq, tm):
    t = proj.shape[0]
    nsb = seq // tm
    cb = COL_MLA // MLA_W
    full = lambda a: _layer_spec(a, l)
    tab = pl.BlockSpec((tm, HEAD_PAD), lambda i: (i % nsb, 0))
    qk_w = MLA_HEADS * HEAD_PAD
    return pl.pallas_call(
        _mla_prep_kernel,
        grid=(t // tm,),
        in_specs=[pl.BlockSpec((tm, MLA_W), lambda i: (i, cb)),
                  full(gq), full(wuq), full(gkv), full(wkc), full(wuv), full(gqq), full(gqk),
                  tab, tab],
        out_specs=[pl.BlockSpec((tm, qk_w), lambda i: (i, 0)),
                   pl.BlockSpec((tm, qk_w), lambda i: (i, 0)),
                   pl.BlockSpec((tm, MLA_HEADS * MLA_V), lambda i: (i, 0))],
        out_shape=[jax.ShapeDtypeStruct((t, qk_w), BF16),
                   jax.ShapeDtypeStruct((t, qk_w), BF16),
                   jax.ShapeDtypeStruct((t, MLA_HEADS * MLA_V), BF16)],
        compiler_params=_cparams(("parallel",)),
        name="mla_prep",
    )(proj, gq, wuq, gkv, wkc, wuv, gqq, gqk, cos_t, sin_t)


def _mla_attn_kernel(q_ref, k_ref, v_ref, o_ref, *, tq, tk):
    i = pl.program_id(2)
    n_diag = tq // tk
    lane = lax.broadcasted_iota(jnp.int32, (tq, LANES), 1)
    row = lax.broadcasted_iota(jnp.int32, (tq, tk), 0)
    col = lax.broadcasted_iota(jnp.int32, (tq, tk), 1)
    qs = [q_ref[:, hh * HEAD_PAD:(hh + 1) * HEAD_PAD] for hh in range(2)]

    def step(kc, carry, dj):
        k0 = pl.multiple_of(kc * tk, tk)
        vb = v_ref[pl.ds(k0, tk), :]
        out = []
        for hh in range(2):
            m, l, acc = carry[hh]
            kb = k_ref[pl.ds(k0, tk), hh * HEAD_PAD:(hh + 1) * HEAD_PAD]
            s = _dot_nt(qs[hh], kb)
            if dj is not None:
                s = jnp.where(col + dj * tk <= row, s, -jnp.inf)
            m_new = jnp.maximum(m, jnp.max(s, axis=-1, keepdims=True))
            alpha = jnp.exp2(m - m_new)
            p = jnp.exp2(s - m_new)
            l = alpha * l + jnp.sum(p, axis=-1, keepdims=True)
            acc = alpha * acc + _dot(p.astype(BF16), vb)
            out.append((m_new, l, acc))
        return tuple(out)

    one = (jnp.full((tq, 1), -jnp.inf, F32), jnp.zeros((tq, 1), F32), jnp.zeros((tq, LANES), F32))
    carry = step(i * n_diag, (one, one), 0)
    for dj in range(1, n_diag):
        carry = step(i * n_diag + dj, carry, dj)
    carry = lax.fori_loop(0, i * n_diag, lambda kc, cr: step(kc, cr, None), carry)
    outs = [acc / l for (_, l, acc) in carry]
    o_ref[...] = jnp.where(lane < MLA_V, outs[0], outs[1]).astype(o_ref.dtype)


def _mla_attn(q, k, v, batch, seq, tq, tk):
    t = q.shape[0]
    nqb = seq // tq
    return pl.pallas_call(
        functools.partial(_mla_attn_kernel, tq=tq, tk=tk),
        grid=(batch, MLA_HEADS // 2, nqb),
        in_specs=[pl.BlockSpec((tq, 2 * HEAD_PAD), lambda b, p, i: (b * nqb + i, p)),
                  pl.BlockSpec((seq, 2 * HEAD_PAD), lambda b, p, i: (b, p)),
                  pl.BlockSpec((seq, LANES), lambda b, p, i: (b, p))],
        out_specs=pl.BlockSpec((tq, LANES), lambda b, p, i: (b * nqb + i, p)),
        out_shape=jax.ShapeDtypeStruct((t, MLA_HEADS * MLA_V), BF16),
        compiler_params=_cparams(("parallel", "parallel", "arbitrary")),
        name="mla_attn",
    )(q, k, v)


CONV_HALO = 32
CONV_ROWS = 64


SUBLANES = SUBLANES_F32


def _conv_kernel(p_ref, w_ref, b_ref, g_ref, beta_ref, o_ref, halo_ref, v_ref, *, ts):
    @pl.when(pl.program_id(1) == 0)
    def _():
        halo_ref[...] = jnp.zeros_like(halo_ref)

    blk = p_ref[...].astype(F32)
    u = blk[:, :CONV_CH] * _sigmoid(blk[:, CONV_CH:])
    u_ext = jnp.concatenate([halo_ref[...], u], axis=0)
    halo_ref[...] = u[ts - CONV_HALO:]
    n_ext = CONV_HALO + ts
    v_ref[0] = u_ext
    for m in range(1, SUBLANES):
        v_ref[m] = pltpu.roll(u_ext, n_ext - m, 0)
    w = w_ref[...]
    bias = b_ref[...]
    gam, beta = g_ref[...], beta_ref[...]
    first = CONV_HALO - (CONV_K - 1)
    for r in range(ts // CONV_ROWS):
        acc = jnp.zeros((CONV_ROWS, CONV_CH), F32) + bias
        for j in range(CONV_K):
            off = first + j
            start = r * CONV_ROWS + (off // SUBLANES) * SUBLANES
            acc = acc + w[j:j + 1, :] * v_ref[off % SUBLANES, start:start + CONV_ROWS, :]
        mu = jnp.mean(acc, axis=-1, keepdims=True)
        xc = acc - mu
        var = jnp.mean(xc * xc, axis=-1, keepdims=True)
        y = xc * lax.rsqrt(var + LN_EPS) * gam + beta
        o_ref[r * CONV_ROWS:(r + 1) * CONV_ROWS, :] = (y * _sigmoid(y)).astype(o_ref.dtype)


def _conformer_conv(proj, w, b, g, beta, l, batch, seq, ts):
    t = proj.shape[0]
    nsb = seq // ts
    cb = COL_CU // (2 * CONV_CH)
    full = lambda a: _layer_spec(a, l)
    return pl.pallas_call(
        functools.partial(_conv_kernel, ts=ts),
        grid=(batch, nsb),
        in_specs=[pl.BlockSpec((ts, 2 * CONV_CH), lambda bb, s: (bb * nsb + s, cb)),
                  full(w), full(b), full(g), full(beta)],
        out_specs=pl.BlockSpec((ts, CONV_CH), lambda bb, s: (bb * nsb + s, 0)),
        out_shape=jax.ShapeDtypeStruct((t, CONV_CH), BF16),
        scratch_shapes=[pltpu.VMEM((CONV_HALO, CONV_CH), F32),
                        pltpu.VMEM((SUBLANES, CONV_HALO + ts, CONV_CH), F32)],
        compiler_params=_cparams(("parallel", "arbitrary")),
        name="conformer_conv",
    )(proj, w, b, g, beta)


SB_SUB = 256


def _sb_attn_kernel(q_ref, k_ref, v_ref, o_ref, *, tq, tk):
    i = pl.program_id(2)
    n_diag = tq // tk
    n_sub = tk // SB_SUB
    lane = lax.broadcasted_iota(jnp.int32, (tq, LANES), 1)
    row = lax.broadcasted_iota(jnp.int32, (tq, SB_SUB), 0)
    col = lax.broadcasted_iota(jnp.int32, (tq, SB_SUB), 1)
    r2 = lax.broadcasted_iota(jnp.int32, (SB_SUB, SB_SUB), 0)
    c2 = lax.broadcasted_iota(jnp.int32, (SB_SUB, SB_SUB), 1)
    suffix = (r2 >= c2).astype(BF16)
    q = q_ref[...]
    qs = [jnp.where((lane >= hh * SB_HEAD_DIM) & (lane < (hh + 1) * SB_HEAD_DIM), q,
                    jnp.zeros_like(q)) for hh in range(2)]

    def step(kc, carry, dj):
        k0 = pl.multiple_of(kc * tk, tk)
        kb = k_ref[pl.ds(k0, tk), :]
        vb = v_ref[pl.ds(k0, tk), :]
        out = []
        for hh in range(2):
            run, acc = carry[hh]
            z_all = _dot_nt(qs[hh], kb)
            wgts = [None] * n_sub
            for sub in reversed(range(n_sub)):
                z = z_all[:, sub * SB_SUB:(sub + 1) * SB_SUB]
                sp = jnp.maximum(z, 0.0) + jnp.log(1.0 + jnp.exp2(_neg_abs(z))) * LOG2E
                if dj is not None:
                    valid = col + (dj * tk + sub * SB_SUB) < row
                    sp = jnp.where(valid, sp, 0.0)
                incl = _dot(sp.astype(BF16), suffix)
                wgt = jnp.exp2(z - incl - run)
                if dj is not None:
                    wgt = jnp.where(valid, wgt, 0.0)
                wgts[sub] = wgt.astype(BF16)
                run = run + incl[:, :1]
            wcat = wgts[0] if n_sub == 1 else jnp.concatenate(wgts, axis=1)
            out.append((run, acc + _dot(wcat, vb)))
        return tuple(out)

    one = (jnp.zeros((tq, 1), F32), jnp.zeros((tq, LANES), F32))
    carry = (one, one)
    for dj in reversed(range(n_diag)):
        carry = step(i * n_diag + dj, carry, dj)
    nfull = i * n_diag
    carry = lax.fori_loop(0, nfull, lambda it, cr: step(nfull - 1 - it, cr, None), carry)
    o_ref[...] = jnp.where(lane < SB_HEAD_DIM, carry[0][1], carry[1][1]).astype(o_ref.dtype)


def _sb_attn(proj, batch, seq, tq, tk):
    t = proj.shape[0]
    nqb = seq // tq
    qc, kc, vc = (COL_SB // LANES, (COL_SB + 256) // LANES, (COL_SB + 512) // LANES)
    return pl.pallas_call(
        functools.partial(_sb_attn_kernel, tq=tq, tk=tk),
        grid=(batch, SB_HEADS // 2, nqb),
        in_specs=[pl.BlockSpec((tq, LANES), lambda b, p, i: (b * nqb + i, qc + p)),
                  pl.BlockSpec((seq, LANES), lambda b, p, i: (b, kc + p)),
                  pl.BlockSpec((seq, LANES), lambda b, p, i: (b, vc + p))],
        out_specs=pl.BlockSpec((tq, LANES), lambda b, p, i: (b * nqb + i, p)),
        out_shape=jax.ShapeDtypeStruct((t, SB_HEADS * SB_HEAD_DIM), BF16),
        compiler_params=_cparams(("parallel", "parallel", "arbitrary")),
        name="sb_attn",
    )(proj, proj, proj)


def _merge_kernel(x_ref, gl_ref, ya_ref, yb_ref, yc_ref, yd_ref, wb_ref, wo_ref, o_ref):
    merged = None
    for n, y_ref in enumerate((ya_ref, yb_ref, yc_ref, yd_ref)):
        hb = _dot(y_ref[...], wb_ref[n])
        th = jnp.tanh(gl_ref[:, n * D_MODEL:(n + 1) * D_MODEL].astype(F32))
        term = hb + hb * th
        merged = term if merged is None else merged + term
    o_ref[...] = x_ref[...] + _dot(merged.astype(BF16), wo_ref[...])


def _merge(x2d, proj, ya, yb, yc, yd, wb, wo, l, tm):
    t, d = x2d.shape
    yspec = pl.BlockSpec((tm, BRANCH_W), lambda i: (i, 0))
    return pl.pallas_call(
        _merge_kernel,
        grid=(t // tm,),
        in_specs=[pl.BlockSpec((tm, d), lambda i: (i, 0)),
                  pl.BlockSpec((tm, N_BRANCH * d), lambda i: (i, COL_GATE // (N_BRANCH * d))),
                  yspec, yspec, yspec, yspec,
                  _layer_spec(wb, l), _layer_spec(wo, l)],
        out_specs=pl.BlockSpec((tm, d), lambda i: (i, 0)),
        out_shape=jax.ShapeDtypeStruct((t, d), F32),
        compiler_params=_cparams(("parallel",)),
        name="merge_out",
    )(x2d, proj, ya, yb, yc, yd, wb, wo)


FFN_HALO = 16
FFN_ROW_SPLIT = 8


def _ffn_kernel(x_ref, xh_ref, g_ref, wv_ref, wg_ref, cwv_ref, cwg_ref, cbv_ref, cbg_ref,
                wd_ref, o_ref, h_ref, acc_ref, act_a, act_b, *, nk, n_row_blocks, blocks_per_seq):
    s = pl.program_id(0)
    c = s % nk
    i = jnp.minimum(s // nk, n_row_blocks - 1)

    def norm(x):
        ms = jnp.mean(x * x, axis=-1, keepdims=True)
        return (x * lax.rsqrt(ms + EPS) * g_ref[...]).astype(BF16)

    @pl.when(s == 0)
    def _():
        act_a[...] = jnp.zeros_like(act_a)
        acc_ref[...] = jnp.zeros_like(acc_ref)

    @pl.when(c == 0)
    def _():
        h_ref[FFN_HALO:, :] = norm(x_ref[...])
        keep = (i % blocks_per_seq != 0).astype(F32)
        h_ref[0:FFN_HALO, :] = norm(xh_ref[...] * keep)

    def conv(ext, cw_ref, cb_ref):
        cw = cw_ref[...]
        u1 = pltpu.roll(ext, 1, 0)[FFN_HALO:]
        u2 = pltpu.roll(ext, 2, 0)[FFN_HALO:]
        return cw[2:3] * ext[FFN_HALO:] + cw[1:2] * u1 + cw[0:1] * u2 + cb_ref[...]

    tm = acc_ref.shape[0]
    rs = tm // FFN_ROW_SPLIT

    def main(act_in, act_out):
        tails = None
        for j in range(FFN_ROW_SPLIT):
            lo = FFN_HALO + j * rs
            if j == 0:
                hs = h_ref[0:lo + rs, :]
                exts = [_dot(hs, w_ref[...]) for w_ref in (wv_ref, wg_ref)]
            else:
                hs = h_ref[lo:lo + rs, :]
                exts = [jnp.concatenate([t, _dot(hs, w_ref[...])], axis=0)
                        for t, w_ref in zip(tails, (wv_ref, wg_ref))]
            tails = [e[rs:] for e in exts]
            val = conv(exts[0], cwv_ref, cbv_ref)
            hg = conv(exts[1], cwg_ref, cbg_ref)
            act_out[j * rs:(j + 1) * rs, :] = ((hg + hg * jnp.tanh(hg)) * val).astype(BF16)
        acc_ref[...] += _dot(act_in[...], wd_ref[...])

    @pl.when(s % 2 == 0)
    def _():
        main(act_a, act_b)

    @pl.when(s % 2 == 1)
    def _():
        main(act_b, act_a)

    @pl.when(c == 0)
    def _():
        o_ref[...] = acc_ref[...]
        acc_ref[...] = x_ref[...]


def _ffn(x2d, g, w_up, cw, cb, w_down, l, seq, tm):
    t, d = x2d.shape
    tk = w_up.shape[3]
    nk = D_FF // tk
    hb = tm // FFN_HALO
    nrb = t // tm
    row_blk = lambda s: jnp.minimum(s // nk, nrb - 1)
    return pl.pallas_call(
        functools.partial(_ffn_kernel, nk=nk, n_row_blocks=nrb, blocks_per_seq=seq // tm),
        grid=(nrb * nk + 1,),
        in_specs=[pl.BlockSpec((tm, d), lambda s: (row_blk(s), 0)),
                  pl.BlockSpec((FFN_HALO, d), lambda s: (jnp.maximum(row_blk(s) * hb - 1, 0), 0)),
                  _layer_spec(g, l),
                  pl.BlockSpec((None, None, d, tk), lambda s: (l, s % nk, 0, 0)),
                  pl.BlockSpec((None, None, d, tk), lambda s: (l, nk + s % nk, 0, 0)),
                  pl.BlockSpec((None, FFN_CONV_K, tk), lambda s: (l, 0, s % nk)),
                  pl.BlockSpec((None, FFN_CONV_K, tk), lambda s: (l, 0, nk + s % nk)),
                  pl.BlockSpec((None, 1, tk), lambda s: (l, 0, s % nk)),
                  pl.BlockSpec((None, 1, tk), lambda s: (l, 0, nk + s % nk)),
                  pl.BlockSpec((None, tk, d), lambda s: (l, (s + nk - 1) % nk, 0))],
        out_specs=pl.BlockSpec((tm, d), lambda s: (jnp.maximum(s - 1, 0) // nk, 0)),
        out_shape=jax.ShapeDtypeStruct((t, d), F32),
        scratch_shapes=[pltpu.VMEM((tm + FFN_HALO, d), BF16), pltpu.VMEM((tm, d), F32),
                        pltpu.VMEM((tm, tk), BF16), pltpu.VMEM((tm, tk), BF16)],
        compiler_params=_cparams(("arbitrary",)),
        name="ffn",
    )(x2d, x2d, g, w_up, w_up, cw, cw, cb, cb, w_down)


def _prep_w_in(w_in):
    offs = np.concatenate([[0], np.cumsum(IN_COLS)])
    seg = lambda i, c=1.0: (w_in[..., offs[i]:offs[i + 1]] * c).astype(BF16)
    pad = jnp.zeros(w_in.shape[:-1] + (MLA_W - MLA_Q_RANK - MLA_KV_RANK - MLA_ROPE,), BF16)
    parts = [seg(11, 0.5), seg(0), seg(1), seg(2), seg(3), seg(4), seg(5), seg(6), pad,
             seg(7), seg(8, SB_HEAD_DIM ** -0.5 * LOG2E), seg(9), seg(10)]
    return jnp.concatenate(parts, axis=-1)


def _prep_mla_weights(w_uq, w_ukv):
    nl = w_uq.shape[0]
    wq = w_uq.reshape(nl, MLA_Q_RANK, MLA_HEADS, MLA_QK)
    wq = jnp.pad(wq, ((0, 0), (0, 0), (0, 0), (0, HEAD_PAD - MLA_QK)))
    wq = wq.reshape(nl, MLA_Q_RANK, MLA_HEADS * HEAD_PAD)
    wkv = w_ukv.reshape(nl, MLA_KV_RANK, MLA_HEADS, MLA_NOPE + MLA_V)
    wk = jnp.pad(wkv[..., :MLA_NOPE], ((0, 0), (0, 0), (0, 0), (0, HEAD_PAD - MLA_NOPE)))
    wk = wk.reshape(nl, MLA_KV_RANK, MLA_HEADS * HEAD_PAD)
    wv = wkv[..., MLA_NOPE:].reshape(nl, MLA_KV_RANK, MLA_HEADS * MLA_V)
    place = np.zeros((LANES, MLA_HEADS * HEAD_PAD), np.float32)
    for h in range(MLA_HEADS):
        for r in range(MLA_ROPE):
            place[r, h * HEAD_PAD + MLA_NOPE + r] = 1.0
    wkc = jnp.concatenate([wk, jnp.broadcast_to(jnp.asarray(place), (nl,) + place.shape)], axis=1)
    wq = jnp.concatenate([wq, _swap_rope_lanes(wq)], axis=-1)
    wkc = jnp.concatenate([wkc, _swap_rope_lanes(wkc)], axis=-1)
    return wq.astype(BF16), wkc.astype(BF16), wv.astype(BF16)


def _swap_rope_lanes(a):
    half = MLA_ROPE // 2
    perm = np.arange(HEAD_PAD)
    perm[MLA_NOPE:MLA_NOPE + half] = np.arange(MLA_NOPE + half, MLA_QK)
    perm[MLA_NOPE + half:MLA_QK] = np.arange(MLA_NOPE, MLA_NOPE + half)
    groups = a.shape[-1] // HEAD_PAD
    idx = (np.arange(groups)[:, None] * HEAD_PAD + perm[None, :]).reshape(-1)
    return a[..., idx]


def _qk_gain(g):
    gp = jnp.pad(g, ((0, 0), (0, HEAD_PAD - MLA_QK)))
    return jnp.concatenate([gp, _swap_rope_lanes(gp)], axis=-1)[:, None, :]


def _rope_tables(seq):
    half = MLA_ROPE // 2
    freqs = ROPE_BASE ** (-jnp.arange(half, dtype=F32) / half)
    ang = jnp.arange(seq, dtype=F32)[:, None] * freqs[None, :]
    cos, sin = jnp.cos(ang), jnp.sin(ang)
    tail = HEAD_PAD - MLA_QK
    cos_t = jnp.concatenate([jnp.ones((seq, MLA_NOPE), F32), cos, cos, jnp.ones((seq, tail), F32)],
                            axis=1)
    sin_t = jnp.concatenate([jnp.zeros((seq, MLA_NOPE), F32), -sin, sin, jnp.zeros((seq, tail), F32)],
                            axis=1)
    return cos_t, sin_t


def kernel(x, g_mix, w_in, lb_logits, g_hg_out, g_q_lat, w_uq, g_kv_lat, w_ukv, g_qk_q, g_qk_k,
           conv_w, conv_b, conv_ln_g, conv_ln_b, w_branch, w_out, g_ffn, w_up, ffn_conv_w,
           ffn_conv_b, w_down):
    batch, seq, d = x.shape
    depth = w_in.shape[0]
    t = batch * seq

    tm_proj = min(1024, seq)
    ts_hg = min(512, seq)
    tm_mla = min(512, seq)
    tq_attn = min(1024, seq)
    tk_attn = min(1024, seq)
    ts_conv = min(512, seq)
    tm_merge = min(512, seq)
    tm_ffn = min(1024, seq)

    w_in_p = _col_chunks(_prep_w_in(w_in), TN_PROJ)
    wq_p, wkc_p, wv_p = _prep_mla_weights(w_uq, w_ukv)
    cos_t, sin_t = _rope_tables(seq)
    rows = lambda a: a[:, None, :]
    gqq_p = _qk_gain(g_qk_q)
    gqk_p = _qk_gain(g_qk_k)
    gout_t = rows(jnp.tile(g_hg_out, (1, HG_HEADS)))
    w_branch_b = (0.5 * w_branch).astype(BF16)
    w_out_b = w_out.astype(BF16)
    w_up_b = _col_chunks(w_up.astype(BF16), TK_FFN)
    w_down_b = w_down.astype(BF16)
    ids = np.arange(depth)
    lmasks = jnp.asarray(((ids[None, :] >= 1) & (ids[None, :] <= ids[:, None]))
                         .astype(np.float32)[:, :, None])
    lb_logits = lb_logits.astype(F32)
    g_mix3, g_ffn3, g_q3, g_kv3 = rows(g_mix), rows(g_ffn), rows(g_q_lat), rows(g_kv_lat)
    conv_b3, ln_g3, ln_b3 = rows(conv_b), rows(conv_ln_g), rows(conv_ln_b)
    gate_half = jnp.asarray(np.concatenate([np.ones(D_FF, np.float32), np.full(D_FF, 0.5, np.float32)]))
    ffn_cw = ffn_conv_w * gate_half
    ffn_b3 = rows(ffn_conv_b * gate_half)

    x2d = x.reshape(t, d)
    for l in range(depth):
        proj = _norm_matmul(x2d, g_mix3, w_in_p, l, tm_proj)
        y_a = _hgrn(proj, lb_logits, lmasks, gout_t, l, batch, seq, ts_hg)
        q, k, v = _mla_prep(proj, g_q3, wq_p, g_kv3, wkc_p, wv_p, gqq_p, gqk_p,
                            cos_t, sin_t, l, seq, tm_mla)
        y_b = _mla_attn(q, k, v, batch, seq, tq_attn, tk_attn)
        y_c = _conformer_conv(proj, conv_w, conv_b3, ln_g3, ln_b3, l, batch, seq, ts_conv)
        y_d = _sb_attn(proj, batch, seq, tq_attn, tk_attn)
        x2d = _merge(x2d, proj, y_a, y_b, y_c, y_d, w_branch_b, w_out_b, l, tm_merge)
        x2d = _ffn(x2d, g_ffn3, w_up_b, ffn_cw, ffn_b3, w_down_b, l, seq, tm_ffn)
    return x2d.reshape(batch, seq, d)
```

```python
import functools

import jax
import jax.numpy as jnp
import numpy as np
from jax import lax
from jax.experimental import pallas as pl
from jax.experimental.pallas import tpu as pltpu

F32 = jnp.float32
BF16 = jnp.bfloat16

D_MODEL = 1024
HG_HEADS = 4
HG_DK = 64
HG_DV = 64
MLA_HEADS = 4
MLA_Q_RANK = 256
MLA_KV_RANK = 128
MLA_NOPE = 64
MLA_ROPE = 32
MLA_V = 64
MLA_QK = MLA_NOPE + MLA_ROPE
ROPE_BASE = 10000.0
CONV_CH = 256
CONV_K = 31
SB_HEADS = 4
SB_HEAD_DIM = 64
N_BRANCH = 4
BRANCH_W = 256
D_FF = 2816
FFN_CONV_K = 3
EPS = 1e-6
LN_EPS = 1e-5
LOG2E = 1.4426950408889634

IN_COLS = (256, 256, 256, 256, MLA_Q_RANK, MLA_KV_RANK, MLA_ROPE, 2 * CONV_CH, 256, 256, 256,
           N_BRANCH * D_MODEL)

LANES = 128
BF16_ROWS = 16
SUBLANES_F32 = 8
VMEM_LIMIT = 56 * 1024 * 1024

COL_GATE = 0
COL_HG = 4096
COL_MLA = 5120
COL_CU = 5632
COL_SB = 6144
N_IN = 6912
MLA_W = 512
TN_PROJ = 768
TK_FFN = 256
HEAD_PAD = 128

HG_CHUNK = 64
HG_SUB = 16
HG_UNROLL = 4
HG_W = HG_HEADS * HG_DK


def _cparams(sem):
    return pltpu.CompilerParams(dimension_semantics=sem, vmem_limit_bytes=VMEM_LIMIT)


def _sigmoid(x):
    return 1.0 / (1.0 + jnp.exp(-x))


def _neg_abs(x):
    bits = pltpu.bitcast(x, jnp.uint32) | jnp.uint32(0x80000000)
    return pltpu.bitcast(bits, F32)


def _split_bf16(x):
    hi = x.astype(BF16)
    lo = (x - hi.astype(F32)).astype(BF16)
    return hi, lo


def _dot(a, b):
    return jnp.dot(a, b, preferred_element_type=F32)


def _dot_nt(a, b):
    return lax.dot_general(a, b, (((1,), (1,)), ((), ())), preferred_element_type=F32)


def _dot_tn(a, b):
    return lax.dot_general(a, b, (((0,), (0,)), ((), ())), preferred_element_type=F32)


def _norm_matmul_kernel(x_ref, g_ref, w_ref, o_ref, h_ref):
    @pl.when(pl.program_id(1) == 0)
    def _():
        x = x_ref[...]
        ms = jnp.mean(x * x, axis=-1, keepdims=True)
        h_ref[...] = (x * lax.rsqrt(ms + EPS) * g_ref[...]).astype(BF16)

    o_ref[...] = _dot(h_ref[...], w_ref[...]).astype(o_ref.dtype)


def _layer_spec(a, l):
    nd = a.ndim - 1
    return pl.BlockSpec((None,) + a.shape[1:], lambda *_: (l,) + (0,) * nd)


def _norm_matmul(x2d, g, w, l, tm, tn):
    t, d = x2d.shape
    n = w.shape[2]
    return pl.pallas_call(
        _norm_matmul_kernel,
        grid=(t // tm, n // tn),
        in_specs=[pl.BlockSpec((tm, d), lambda i, j: (i, 0)),
                  _layer_spec(g, l),
                  pl.BlockSpec((None, d, tn), lambda i, j: (l, 0, j))],
        out_specs=pl.BlockSpec((tm, tn), lambda i, j: (i, j)),
        out_shape=jax.ShapeDtypeStruct((t, n), BF16),
        scratch_shapes=[pltpu.VMEM((tm, d), BF16)],
        compiler_params=_cparams(("parallel", "arbitrary")),
        name="norm_matmul",
    )(x2d, g, w)


def _hgrn_kernel(p_ref, lbl_ref, lmask_ref, gout_ref, o_ref, st_ref, *, n_chunks):
    @pl.when(pl.program_id(1) == 0)
    def _():
        st_ref[...] = jnp.zeros_like(st_ref)

    c = HG_CHUNK
    w = HG_W
    logits = lbl_ref[...]
    mx = jnp.max(logits, axis=0, keepdims=True)
    ex = jnp.exp(logits - mx)
    sm = ex / jnp.sum(ex, axis=0, keepdims=True)
    lb = jnp.sum(sm * lmask_ref[...], axis=0, keepdims=True)
    log_lb = jnp.log(lb)
    log_1mlb = jnp.log1p(-lb)
    one_mlb = 1.0 - lb
    gout = gout_ref[...]

    row = lax.broadcasted_iota(jnp.int32, (c, c), 0)
    col = lax.broadcasted_iota(jnp.int32, (c, c), 1)
    tri = (col <= row).astype(BF16)
    rmod = lax.broadcasted_iota(jnp.int32, (c, w), 0) % HG_SUB
    rmod8 = lax.broadcasted_iota(jnp.int32, (c, w), 0) % SUBLANES_F32
    blk_t = lax.broadcasted_iota(jnp.int32, (HG_HEADS * c, c), 0) % c // HG_SUB
    blk_s = lax.broadcasted_iota(jnp.int32, (HG_HEADS * c, c), 1) // HG_SUB
    same_block = blk_t == blk_s
    lane_head = lax.broadcasted_iota(jnp.int32, (c, w), 1) // HG_DK
    eh_r = lax.broadcasted_iota(jnp.int32, (w, w), 0) // HG_DK
    eh_c = lax.broadcasted_iota(jnp.int32, (w, w), 1) // HG_DK
    same_head = eh_r == eh_c
    eh = same_head.astype(BF16)
    n_sub = c // HG_SUB
    cat_head = lax.broadcasted_iota(jnp.int32, (c, (n_sub - 1) * w), 1) % w // HG_DK

    def chunk(ci, carry):
        r0 = pl.multiple_of(ci * c, c)
        blk = p_ref[pl.ds(r0, c), :].astype(F32)
        hq, hf, hv, hg = blk[:, :w], blk[:, w:2 * w], blk[:, 2 * w:3 * w], blk[:, 3 * w:]
        q = hq * _sigmoid(hq)
        e = jnp.exp(-jnp.abs(hf))
        l1pe = jnp.log(1.0 + e)
        logsig = jnp.minimum(hf, 0.0) - l1pe
        b_term = log_1mlb + logsig
        mxab = jnp.maximum(log_lb, b_term)
        lf = mxab + jnp.log(1.0 + jnp.exp(-jnp.abs(log_lb - b_term)))
        kk = one_mlb * jnp.where(hf >= 0.0, e, 1.0) / (1.0 + e)
        v_bf = hv.astype(BF16)

        lf_hi, lf_lo = _split_bf16(lf)
        bcum = _dot(tri, lf_hi) + _dot(tri, lf_lo)
        b_last = bcum[c - 1:c, :]

        st = st_ref[...]
        qe = (q * jnp.exp(bcum)).astype(BF16)
        o = _dot_nt(qe, st.astype(BF16))

        qms, kms = [], []
        for i in range(1, n_sub):
            lo_r, hi_r = i * HG_SUB, (i + 1) * HG_SUB
            ref_row = bcum[lo_r - 1:lo_r, :]
            qi = q[lo_r:hi_r] * jnp.exp(bcum[lo_r:hi_r] - ref_row)
            pieces = [jnp.zeros((lo_r, w), F32), qi]
            if hi_r < c:
                pieces.append(jnp.zeros((c - hi_r, w), F32))
            qms.append(jnp.concatenate(pieces, axis=0))
            ki = kk[:lo_r] * jnp.exp(ref_row - bcum[:lo_r])
            kms.append(jnp.concatenate([ki, jnp.zeros((c - lo_r, w), F32)], axis=0))
        qc = jnp.concatenate(qms, axis=1)
        kc = jnp.concatenate(kms, axis=1).astype(BF16)
        qstack = jnp.concatenate(
            [jnp.where(cat_head == h, qc, 0.0) for h in range(HG_HEADS)], axis=0).astype(BF16)
        a_stack = _dot_nt(qstack, kc)

        half = HG_SUB // 2
        ref2 = jnp.concatenate(
            [jnp.broadcast_to(bcum[i * HG_SUB + half - 1:i * HG_SUB + half, :], (HG_SUB, w))
             for i in range(n_sub)], axis=0)
        upper = rmod >= half
        q2 = jnp.where(upper, q * jnp.exp(jnp.where(upper, bcum - ref2, 0.0)), 0.0)
        k2 = jnp.where(upper, 0.0, kk * jnp.exp(jnp.where(upper, 0.0, ref2 - bcum))).astype(BF16)
        q2stack = jnp.concatenate(
            [jnp.where(lane_head == h, q2, 0.0) for h in range(HG_HEADS)], axis=0).astype(BF16)
        a2 = _dot_nt(q2stack, k2)
        a_tot = a_stack + jnp.where(same_block, a2, 0.0)
        r_full = _dot(a_tot.astype(BF16), v_bf)
        for h in range(HG_HEADS):
            o = o + jnp.where(lane_head == h, r_full[h * c:(h + 1) * c], 0.0)

        def shift(a, d):
            a3 = a.reshape(c // SUBLANES_F32, SUBLANES_F32, w)
            return pltpu.roll(a3, d, 1).reshape(c, w)

        for d in range(half):
            if d == 0:
                ks, bs, vs = kk, bcum, hv
            else:
                ks, bs, vs = shift(kk, d), shift(bcum, d), shift(hv, d)
            valid = rmod8 >= d
            pd = jnp.where(valid, q * ks * jnp.exp(jnp.where(valid, bcum - bs, 0.0)), 0.0)
            o = o + _dot(pd.astype(BF16), eh) * vs

        kdec = (kk * jnp.exp(b_last - bcum)).astype(BF16)
        upd = _dot_tn(v_bf, kdec)
        st_ref[...] = st * jnp.exp(b_last) + jnp.where(same_head, upd, 0.0)

        o2_hi, o2_lo = _split_bf16(o * o)
        ms = (_dot(o2_hi, eh) + _dot(o2_lo, eh)) * (1.0 / HG_DV)
        y = o * lax.rsqrt(ms + EPS) * gout * (hg * _sigmoid(hg))
        o_ref[pl.ds(r0, c), :] = y.astype(o_ref.dtype)
        return carry

    def trip(ti, carry):
        for u in range(HG_UNROLL):
            carry = chunk(ti * HG_UNROLL + u, carry)
        return carry

    lax.fori_loop(0, n_chunks // HG_UNROLL, trip, 0)


def _hgrn(proj, lb_logits, lmasks, gout_t, l, batch, seq, ts):
    t = proj.shape[0]
    nsb = seq // ts
    cb = COL_HG // (4 * HG_W)
    return pl.pallas_call(
        functools.partial(_hgrn_kernel, n_chunks=ts // HG_CHUNK),
        grid=(batch, nsb),
        in_specs=[pl.BlockSpec((ts, 4 * HG_W), lambda b, s: (b * nsb + s, cb)),
                  pl.BlockSpec(lb_logits.shape, lambda b, s: (0, 0)),
                  _layer_spec(lmasks, l),
                  _layer_spec(gout_t, l)],
        out_specs=pl.BlockSpec((ts, HG_W), lambda b, s: (b * nsb + s, 0)),
        out_shape=jax.ShapeDtypeStruct((t, HG_W), BF16),
        scratch_shapes=[pltpu.VMEM((HG_W, HG_W), F32)],
        compiler_params=_cparams(("parallel", "arbitrary")),
        name="hgrn2",
    )(proj, lb_logits, lmasks, gout_t)


def _mla_prep_kernel(p_ref, gq_ref, wuq_ref, gkv_ref, wkc_ref, wuv_ref, gqq_ref, gqk_ref,
                     cos_ref, sin_ref, q_ref, k_ref, v_ref):
    qk_w = MLA_HEADS * HEAD_PAD
    blk = p_ref[...].astype(F32)
    cq = blk[:, :MLA_Q_RANK]
    ckv = blk[:, MLA_Q_RANK:MLA_Q_RANK + MLA_KV_RANK]
    krp = blk[:, MLA_Q_RANK + MLA_KV_RANK:]

    def rms(x, g):
        ms = jnp.mean(x * x, axis=-1, keepdims=True)
        return x * lax.rsqrt(ms + EPS) * g

    cqn = rms(cq, gq_ref[...]).astype(BF16)
    ckvn = rms(ckv, gkv_ref[...]).astype(BF16)
    q_raw = _dot(cqn, wuq_ref[...])
    k_raw = _dot(jnp.concatenate([ckvn, krp.astype(BF16)], axis=1), wkc_ref[...])
    v_ref[...] = _dot(ckvn, wuv_ref[...]).astype(v_ref.dtype)

    cos, sin = cos_ref[...], sin_ref[...]

    def head(raw, h, g_ref, scale):
        x = raw[:, h * HEAD_PAD:(h + 1) * HEAD_PAD]
        xs = raw[:, qk_w + h * HEAD_PAD:qk_w + (h + 1) * HEAD_PAD]
        ms = jnp.sum(x * x, axis=-1, keepdims=True) * (1.0 / MLA_QK)
        r = lax.rsqrt(ms + EPS) * scale
        return (x * g_ref[:, :HEAD_PAD] * cos + xs * g_ref[:, HEAD_PAD:] * sin) * r

    for h in range(MLA_HEADS):
        sl = slice(h * HEAD_PAD, (h + 1) * HEAD_PAD)
        q_ref[:, sl] = head(q_raw, h, gqq_ref, MLA_QK ** -0.5 * LOG2E).astype(q_ref.dtype)
        k_ref[:, sl] = head(k_raw, h, gqk_ref, 1.0).astype(k_ref.dtype)


def _mla_prep(proj, gq, wuq, gkv, wkc, wuv, gqq, gqk, cos_t, sin_t, l, seq, tm):
    t = proj.shape[0]
    nsb = seq // tm
    cb = COL_MLA // MLA_W
    full = lambda a: _layer_spec(a, l)
    tab = pl.BlockSpec((tm, HEAD_PAD), lambda i: (i % nsb, 0))
    qk_w = MLA_HEADS * HEAD_PAD
    return pl.pallas_call(
        _mla_prep_kernel,
        grid=(t // tm,),
        in_specs=[pl.BlockSpec((tm, MLA_W), lambda i: (i, cb)),
                  full(gq), full(wuq), full(gkv), full(wkc), full(wuv), full(gqq), full(gqk),
                  tab, tab],
        out_specs=[pl.BlockSpec((tm, qk_w), lambda i: (i, 0)),
                   pl.BlockSpec((tm, qk_w), lambda i: (i, 0)),
                   pl.BlockSpec((tm, MLA_HEADS * MLA_V), lambda i: (i, 0))],
        out_shape=[jax.ShapeDtypeStruct((t, qk_w), BF16),
                   jax.ShapeDtypeStruct((t, qk_w), BF16),
                   jax.ShapeDtypeStruct((t, MLA_HEADS * MLA_V), BF16)],
        compiler_params=_cparams(("parallel",)),
        name="mla_prep",
    )(proj, gq, wuq, gkv, wkc, wuv, gqq, gqk, cos_t, sin_t)


def _mla_attn_kernel(q_ref, k_ref, v_ref, o_ref, *, tq, tk):
    i = pl.program_id(2)
    n_diag = tq // tk
    lane = lax.broadcasted_iota(jnp.int32, (tq, LANES), 1)
    qs = [q_ref[:, hh * HEAD_PAD:(hh + 1) * HEAD_PAD] for hh in range(2)]

    row = lax.broadcasted_iota(jnp.int32, (tq, tk), 0)
    col = lax.broadcasted_iota(jnp.int32, (tq, tk), 1)

    def step(kc, carry, dj):
        k0 = pl.multiple_of(kc * tk, tk)
        vb = v_ref[pl.ds(k0, tk), :]
        out = []
        for hh in range(2):
            m, l, acc = carry[hh]
            kb = k_ref[pl.ds(k0, tk), hh * HEAD_PAD:(hh + 1) * HEAD_PAD]
            s = _dot_nt(qs[hh], kb)
            if dj is not None:
                s = jnp.where(col + dj * tk <= row, s, -jnp.inf)
            m_new = jnp.maximum(m, jnp.max(s, axis=-1, keepdims=True))
            alpha = jnp.exp2(m - m_new)
            p = jnp.exp2(s - m_new)
            l = alpha * l + jnp.sum(p, axis=-1, keepdims=True)
            acc = alpha * acc + _dot(p.astype(BF16), vb)
            out.append((m_new, l, acc))
        return tuple(out)

    one = (jnp.full((tq, 1), -jnp.inf, F32), jnp.zeros((tq, 1), F32), jnp.zeros((tq, LANES), F32))
    carry = step(i * n_diag, (one, one), 0)
    for dj in range(1, n_diag):
        carry = step(i * n_diag + dj, carry, dj)
    carry = lax.fori_loop(0, i * n_diag, lambda kc, cr: step(kc, cr, None), carry)
    outs = [acc / l for (_, l, acc) in carry]
    o_ref[...] = jnp.where(lane < MLA_V, outs[0], outs[1]).astype(o_ref.dtype)


def _mla_attn(q, k, v, batch, seq, tq, tk):
    t = q.shape[0]
    nqb = seq // tq
    return pl.pallas_call(
        functools.partial(_mla_attn_kernel, tq=tq, tk=tk),
        grid=(batch, MLA_HEADS // 2, nqb),
        in_specs=[pl.BlockSpec((tq, 2 * HEAD_PAD), lambda b, p, i: (b * nqb + i, p)),
                  pl.BlockSpec((seq, 2 * HEAD_PAD), lambda b, p, i: (b, p)),
                  pl.BlockSpec((seq, LANES), lambda b, p, i: (b, p))],
        out_specs=pl.BlockSpec((tq, LANES), lambda b, p, i: (b * nqb + i, p)),
        out_shape=jax.ShapeDtypeStruct((t, MLA_HEADS * MLA_V), BF16),
        compiler_params=_cparams(("parallel", "parallel", "arbitrary")),
        name="mla_attn",
    )(q, k, v)


CONV_HALO = 32
CONV_ROWS = 64


SUBLANES = SUBLANES_F32


def _conv_kernel(p_ref, w_ref, b_ref, g_ref, beta_ref, o_ref, halo_ref, v_ref, *, ts):
    @pl.when(pl.program_id(1) == 0)
    def _():
        halo_ref[...] = jnp.zeros_like(halo_ref)

    blk = p_ref[...].astype(F32)
    u = blk[:, :CONV_CH] * _sigmoid(blk[:, CONV_CH:])
    u_ext = jnp.concatenate([halo_ref[...], u], axis=0)
    halo_ref[...] = u[ts - CONV_HALO:]
    n_ext = CONV_HALO + ts
    v_ref[0] = u_ext
    for m in range(1, SUBLANES):
        v_ref[m] = pltpu.roll(u_ext, n_ext - m, 0)
    w = w_ref[...]
    bias = b_ref[...]
    gam, beta = g_ref[...], beta_ref[...]
    first = CONV_HALO - (CONV_K - 1)
    for r in range(ts // CONV_ROWS):
        acc = jnp.zeros((CONV_ROWS, CONV_CH), F32) + bias
        for j in range(CONV_K):
            off = first + j
            start = r * CONV_ROWS + (off // SUBLANES) * SUBLANES
            acc = acc + w[j:j + 1, :] * v_ref[off % SUBLANES, start:start + CONV_ROWS, :]
        mu = jnp.mean(acc, axis=-1, keepdims=True)
        xc = acc - mu
        var = jnp.mean(xc * xc, axis=-1, keepdims=True)
        y = xc * lax.rsqrt(var + LN_EPS) * gam + beta
        o_ref[r * CONV_ROWS:(r + 1) * CONV_ROWS, :] = (y * _sigmoid(y)).astype(o_ref.dtype)


def _conformer_conv(proj, w, b, g, beta, l, batch, seq, ts):
    t = proj.shape[0]
    nsb = seq // ts
    cb = COL_CU // (2 * CONV_CH)
    full = lambda a: _layer_spec(a, l)
    return pl.pallas_call(
        functools.partial(_conv_kernel, ts=ts),
        grid=(batch, nsb),
        in_specs=[pl.BlockSpec((ts, 2 * CONV_CH), lambda bb, s: (bb * nsb + s, cb)),
                  full(w), full(b), full(g), full(beta)],
        out_specs=pl.BlockSpec((ts, CONV_CH), lambda bb, s: (bb * nsb + s, 0)),
        out_shape=jax.ShapeDtypeStruct((t, CONV_CH), BF16),
        scratch_shapes=[pltpu.VMEM((CONV_HALO, CONV_CH), F32),
                        pltpu.VMEM((SUBLANES, CONV_HALO + ts, CONV_CH), F32)],
        compiler_params=_cparams(("parallel", "arbitrary")),
        name="conformer_conv",
    )(proj, w, b, g, beta)


SB_SUB = 256


def _sb_attn_kernel(q_ref, k_ref, v_ref, o_ref, *, tq, tk):
    i = pl.program_id(2)
    n_diag = tq // tk
    n_sub = tk // SB_SUB
    lane = lax.broadcasted_iota(jnp.int32, (tq, LANES), 1)
    r2 = lax.broadcasted_iota(jnp.int32, (SB_SUB, SB_SUB), 0)
    c2 = lax.broadcasted_iota(jnp.int32, (SB_SUB, SB_SUB), 1)
    suffix = (r2 >= c2).astype(BF16)
    q = q_ref[...]
    qs = [jnp.where((lane >= hh * SB_HEAD_DIM) & (lane < (hh + 1) * SB_HEAD_DIM), q,
                    jnp.zeros_like(q)) for hh in range(2)]

    def softplus2(z):
        return jnp.maximum(z, 0.0) + jnp.log(1.0 + jnp.exp2(_neg_abs(z))) * LOG2E

    def step(kc, carry):
        k0 = pl.multiple_of(kc * tk, tk)
        kb = k_ref[pl.ds(k0, tk), :]
        vb = v_ref[pl.ds(k0, tk), :]
        out = []
        for hh in range(2):
            run, acc = carry[hh]
            z_all = _dot_nt(qs[hh], kb)
            wgts = [None] * n_sub
            for sub in reversed(range(n_sub)):
                z = z_all[:, sub * SB_SUB:(sub + 1) * SB_SUB]
                incl = _dot(softplus2(z).astype(BF16), suffix)
                wgts[sub] = jnp.exp2(z - incl - run).astype(BF16)
                run = run + incl[:, :1]
            wcat = wgts[0] if n_sub == 1 else jnp.concatenate(wgts, axis=1)
            out.append((run, acc + _dot(wcat, vb)))
        return tuple(out)

    def diag_step(kc, carry, dj):
        k0 = pl.multiple_of(kc * tk, tk)
        out = []
        for hh in range(2):
            run, acc = carry[hh]
            for sub in reversed(range(n_sub)):
                r0 = dj * tk + sub * SB_SUB
                if r0 >= tq:
                    continue
                nr = tq - r0
                kb = k_ref[pl.ds(k0 + sub * SB_SUB, SB_SUB), :]
                vb = v_ref[pl.ds(k0 + sub * SB_SUB, SB_SUB), :]
                z = _dot_nt(qs[hh][r0:], kb)
                valid = (lax.broadcasted_iota(jnp.int32, (nr, SB_SUB), 1)
                         < lax.broadcasted_iota(jnp.int32, (nr, SB_SUB), 0))
                incl = _dot(jnp.where(valid, softplus2(z), 0.0).astype(BF16), suffix)
                wgt = jnp.where(valid, jnp.exp2(z - incl - run[r0:]), 0.0).astype(BF16)
                run_lo = run[r0:] + incl[:, :1]
                acc_lo = acc[r0:] + _dot(wgt, vb)
                if r0 > 0:
                    run = jnp.concatenate([run[:r0], run_lo], axis=0)
                    acc = jnp.concatenate([acc[:r0], acc_lo], axis=0)
                else:
                    run, acc = run_lo, acc_lo
            out.append((run, acc))
        return tuple(out)

    one = (jnp.zeros((tq, 1), F32), jnp.zeros((tq, LANES), F32))
    carry = (one, one)
    for dj in reversed(range(n_diag)):
        carry = diag_step(i * n_diag + dj, carry, dj)
    nfull = i * n_diag
    carry = lax.fori_loop(0, nfull, lambda it, cr: step(nfull - 1 - it, cr), carry)
    o_ref[...] = jnp.where(lane < SB_HEAD_DIM, carry[0][1], carry[1][1]).astype(o_ref.dtype)


def _sb_attn(proj, batch, seq, tq, tk):
    t = proj.shape[0]
    nqb = seq // tq
    qc, kc, vc = (COL_SB // LANES, (COL_SB + 256) // LANES, (COL_SB + 512) // LANES)
    return pl.pallas_call(
        functools.partial(_sb_attn_kernel, tq=tq, tk=tk),
        grid=(batch, SB_HEADS // 2, nqb),
        in_specs=[pl.BlockSpec((tq, LANES), lambda b, p, i: (b * nqb + i, qc + p)),
                  pl.BlockSpec((seq, LANES), lambda b, p, i: (b, kc + p)),
                  pl.BlockSpec((seq, LANES), lambda b, p, i: (b, vc + p))],
        out_specs=pl.BlockSpec((tq, LANES), lambda b, p, i: (b * nqb + i, p)),
        out_shape=jax.ShapeDtypeStruct((t, SB_HEADS * SB_HEAD_DIM), BF16),
        compiler_params=_cparams(("parallel", "parallel", "arbitrary")),
        name="sb_attn",
    )(proj, proj, proj)


def _merge_kernel(x_ref, gl_ref, ya_ref, yb_ref, yc_ref, yd_ref, wb_ref, wo_ref, o_ref):
    merged = None
    for n, y_ref in enumerate((ya_ref, yb_ref, yc_ref, yd_ref)):
        hb = _dot(y_ref[...], wb_ref[n])
        th = jnp.tanh(gl_ref[:, n * D_MODEL:(n + 1) * D_MODEL].astype(F32))
        term = hb + hb * th
        merged = term if merged is None else merged + term
    o_ref[...] = x_ref[...] + _dot(merged.astype(BF16), wo_ref[...])


def _merge(x2d, proj, ya, yb, yc, yd, wb, wo, l, tm):
    t, d = x2d.shape
    yspec = pl.BlockSpec((tm, BRANCH_W), lambda i: (i, 0))
    return pl.pallas_call(
        _merge_kernel,
        grid=(t // tm,),
        in_specs=[pl.BlockSpec((tm, d), lambda i: (i, 0)),
                  pl.BlockSpec((tm, N_BRANCH * d), lambda i: (i, COL_GATE // (N_BRANCH * d))),
                  yspec, yspec, yspec, yspec,
                  _layer_spec(wb, l), _layer_spec(wo, l)],
        out_specs=pl.BlockSpec((tm, d), lambda i: (i, 0)),
        out_shape=jax.ShapeDtypeStruct((t, d), F32),
        compiler_params=_cparams(("parallel",)),
        name="merge_out",
    )(x2d, proj, ya, yb, yc, yd, wb, wo)


FFN_HALO = 16
FFN_ROW_SPLIT = 1


def _ffn_kernel(x_ref, xh_ref, g_ref, wv_ref, wg_ref, cwv_ref, cwg_ref, cbv_ref, cbg_ref,
                wd_ref, o_ref, h_ref, acc_ref, act_a, act_b, *, nk, n_row_blocks, blocks_per_seq):
    s = pl.program_id(0)
    c = s % nk
    i = jnp.minimum(s // nk, n_row_blocks - 1)

    def norm(x):
        ms = jnp.mean(x * x, axis=-1, keepdims=True)
        return (x * lax.rsqrt(ms + EPS) * g_ref[...]).astype(BF16)

    @pl.when(s == 0)
    def _():
        act_a[...] = jnp.zeros_like(act_a)
        acc_ref[...] = jnp.zeros_like(acc_ref)

    @pl.when(c == 0)
    def _():
        h_ref[FFN_HALO:, :] = norm(x_ref[...])
        keep = (i % blocks_per_seq != 0).astype(F32)
        h_ref[0:FFN_HALO, :] = norm(xh_ref[...] * keep)

    def conv(ext, cw_ref, cb_ref):
        cw = cw_ref[...]
        u1 = pltpu.roll(ext, 1, 0)[FFN_HALO:]
        u2 = pltpu.roll(ext, 2, 0)[FFN_HALO:]
        return cw[2:3] * ext[FFN_HALO:] + cw[1:2] * u1 + cw[0:1] * u2 + cb_ref[...]

    tm = acc_ref.shape[0]
    rs = tm // FFN_ROW_SPLIT

    def main(act_in, act_out):
        tails = None
        for j in range(FFN_ROW_SPLIT):
            lo = FFN_HALO + j * rs
            if j == 0:
                hs = h_ref[0:lo + rs, :]
                exts = [_dot(hs, w_ref[...]) for w_ref in (wv_ref, wg_ref)]
            else:
                hs = h_ref[lo:lo + rs, :]
                exts = [jnp.concatenate([t, _dot(hs, w_ref[...])], axis=0)
                        for t, w_ref in zip(tails, (wv_ref, wg_ref))]
            tails = [e[rs:] for e in exts]
            val = conv(exts[0], cwv_ref, cbv_ref)
            hg = conv(exts[1], cwg_ref, cbg_ref)
            act_out[j * rs:(j + 1) * rs, :] = ((hg + hg * jnp.tanh(hg)) * val).astype(BF16)
        acc_ref[...] += _dot(act_in[...], wd_ref[...])

    @pl.when(s % 2 == 0)
    def _():
        main(act_a, act_b)

    @pl.when(s % 2 == 1)
    def _():
        main(act_b, act_a)

    @pl.when(c == 0)
    def _():
        o_ref[...] = acc_ref[...]
        acc_ref[...] = x_ref[...]


def _ffn(x2d, g, w_up, cw, cb, w_down, l, seq, tm, tk):
    t, d = x2d.shape
    nk = D_FF // tk
    hb = tm // FFN_HALO
    nrb = t // tm
    row_blk = lambda s: jnp.minimum(s // nk, nrb - 1)
    return pl.pallas_call(
        functools.partial(_ffn_kernel, nk=nk, n_row_blocks=nrb, blocks_per_seq=seq // tm),
        grid=(nrb * nk + 1,),
        in_specs=[pl.BlockSpec((tm, d), lambda s: (row_blk(s), 0)),
                  pl.BlockSpec((FFN_HALO, d), lambda s: (jnp.maximum(row_blk(s) * hb - 1, 0), 0)),
                  _layer_spec(g, l),
                  pl.BlockSpec((None, d, tk), lambda s: (l, 0, s % nk)),
                  pl.BlockSpec((None, d, tk), lambda s: (l, 0, nk + s % nk)),
                  pl.BlockSpec((None, FFN_CONV_K, tk), lambda s: (l, 0, s % nk)),
                  pl.BlockSpec((None, FFN_CONV_K, tk), lambda s: (l, 0, nk + s % nk)),
                  pl.BlockSpec((None, 1, tk), lambda s: (l, 0, s % nk)),
                  pl.BlockSpec((None, 1, tk), lambda s: (l, 0, nk + s % nk)),
                  pl.BlockSpec((None, tk, d), lambda s: (l, (s + nk - 1) % nk, 0))],
        out_specs=pl.BlockSpec((tm, d), lambda s: (jnp.maximum(s - 1, 0) // nk, 0)),
        out_shape=jax.ShapeDtypeStruct((t, d), F32),
        scratch_shapes=[pltpu.VMEM((tm + FFN_HALO, d), BF16), pltpu.VMEM((tm, d), F32),
                        pltpu.VMEM((tm, tk), BF16), pltpu.VMEM((tm, tk), BF16)],
        compiler_params=_cparams(("arbitrary",)),
        name="ffn",
    )(x2d, x2d, g, w_up, w_up, cw, cw, cb, cb, w_down)


def _prep_w_in(w_in):
    offs = np.concatenate([[0], np.cumsum(IN_COLS)])
    seg = lambda i, c=1.0: (w_in[..., offs[i]:offs[i + 1]] * c).astype(BF16)
    pad = jnp.zeros(w_in.shape[:-1] + (MLA_W - MLA_Q_RANK - MLA_KV_RANK - MLA_ROPE,), BF16)
    parts = [seg(11, 0.5), seg(0), seg(1), seg(2), seg(3), seg(4), seg(5), seg(6), pad,
             seg(7), seg(8, SB_HEAD_DIM ** -0.5 * LOG2E), seg(9), seg(10)]
    return jnp.concatenate(parts, axis=-1)


def _prep_mla_weights(w_uq, w_ukv):
    nl = w_uq.shape[0]
    wq = w_uq.reshape(nl, MLA_Q_RANK, MLA_HEADS, MLA_QK)
    wq = jnp.pad(wq, ((0, 0), (0, 0), (0, 0), (0, HEAD_PAD - MLA_QK)))
    wq = wq.reshape(nl, MLA_Q_RANK, MLA_HEADS * HEAD_PAD)
    wkv = w_ukv.reshape(nl, MLA_KV_RANK, MLA_HEADS, MLA_NOPE + MLA_V)
    wk = jnp.pad(wkv[..., :MLA_NOPE], ((0, 0), (0, 0), (0, 0), (0, HEAD_PAD - MLA_NOPE)))
    wk = wk.reshape(nl, MLA_KV_RANK, MLA_HEADS * HEAD_PAD)
    wv = wkv[..., MLA_NOPE:].reshape(nl, MLA_KV_RANK, MLA_HEADS * MLA_V)
    place = np.zeros((LANES, MLA_HEADS * HEAD_PAD), np.float32)
    for h in range(MLA_HEADS):
        for r in range(MLA_ROPE):
            place[r, h * HEAD_PAD + MLA_NOPE + r] = 1.0
    wkc = jnp.concatenate([wk, jnp.broadcast_to(jnp.asarray(place), (nl,) + place.shape)], axis=1)
    wq = jnp.concatenate([wq, _swap_rope_lanes(wq)], axis=-1)
    wkc = jnp.concatenate([wkc, _swap_rope_lanes(wkc)], axis=-1)
    return wq.astype(BF16), wkc.astype(BF16), wv.astype(BF16)


def _swap_rope_lanes(a):
    half = MLA_ROPE // 2
    perm = np.arange(HEAD_PAD)
    perm[MLA_NOPE:MLA_NOPE + half] = np.arange(MLA_NOPE + half, MLA_QK)
    perm[MLA_NOPE + half:MLA_QK] = np.arange(MLA_NOPE, MLA_NOPE + half)
    groups = a.shape[-1] // HEAD_PAD
    idx = (np.arange(groups)[:, None] * HEAD_PAD + perm[None, :]).reshape(-1)
    return a[..., idx]


def _qk_gain(g):
    gp = jnp.pad(g, ((0, 0), (0, HEAD_PAD - MLA_QK)))
    return jnp.concatenate([gp, _swap_rope_lanes(gp)], axis=-1)[:, None, :]


def _rope_tables(seq):
    half = MLA_ROPE // 2
    freqs = ROPE_BASE ** (-jnp.arange(half, dtype=F32) / half)
    ang = jnp.arange(seq, dtype=F32)[:, None] * freqs[None, :]
    cos, sin = jnp.cos(ang), jnp.sin(ang)
    tail = HEAD_PAD - MLA_QK
    cos_t = jnp.concatenate([jnp.ones((seq, MLA_NOPE), F32), cos, cos, jnp.ones((seq, tail), F32)],
                            axis=1)
    sin_t = jnp.concatenate([jnp.zeros((seq, MLA_NOPE), F32), -sin, sin, jnp.zeros((seq, tail), F32)],
                            axis=1)
    return cos_t, sin_t


def kernel(x, g_mix, w_in, lb_logits, g_hg_out, g_q_lat, w_uq, g_kv_lat, w_ukv, g_qk_q, g_qk_k,
           conv_w, conv_b, conv_ln_g, conv_ln_b, w_branch, w_out, g_ffn, w_up, ffn_conv_w,
           ffn_conv_b, w_down):
    batch, seq, d = x.shape
    depth = w_in.shape[0]
    t = batch * seq

    tm_proj = min(1024, seq)
    ts_hg = min(512, seq)
    tm_mla = min(512, seq)
    tq_attn = min(1024, seq)
    tk_attn = min(1024, seq)
    ts_conv = min(512, seq)
    tm_merge = min(512, seq)
    tm_ffn = min(1024, seq)

    w_in_p = _prep_w_in(w_in)
    wq_p, wkc_p, wv_p = _prep_mla_weights(w_uq, w_ukv)
    cos_t, sin_t = _rope_tables(seq)
    rows = lambda a: a[:, None, :]
    gqq_p = _qk_gain(g_qk_q)
    gqk_p = _qk_gain(g_qk_k)
    gout_t = rows(jnp.tile(g_hg_out, (1, HG_HEADS)))
    w_branch_b = (0.5 * w_branch).astype(BF16)
    w_out_b = w_out.astype(BF16)
    w_up_b = w_up.astype(BF16)
    w_down_b = w_down.astype(BF16)
    ids = np.arange(depth)
    lmasks = jnp.asarray(((ids[None, :] >= 1) & (ids[None, :] <= ids[:, None]))
                         .astype(np.float32)[:, :, None])
    lb_logits = lb_logits.astype(F32)
    g_mix3, g_ffn3, g_q3, g_kv3 = rows(g_mix), rows(g_ffn), rows(g_q_lat), rows(g_kv_lat)
    conv_b3, ln_g3, ln_b3 = rows(conv_b), rows(conv_ln_g), rows(conv_ln_b)
    gate_half = jnp.asarray(np.concatenate([np.ones(D_FF, np.float32), np.full(D_FF, 0.5, np.float32)]))
    ffn_cw = ffn_conv_w * gate_half
    ffn_b3 = rows(ffn_conv_b * gate_half)

    x2d = x.reshape(t, d)
    for l in range(depth):
        proj = _norm_matmul(x2d, g_mix3, w_in_p, l, tm_proj, TN_PROJ)
        y_a = _hgrn(proj, lb_logits, lmasks, gout_t, l, batch, seq, ts_hg)
        q, k, v = _mla_prep(proj, g_q3, wq_p, g_kv3, wkc_p, wv_p, gqq_p, gqk_p,
                            cos_t, sin_t, l, seq, tm_mla)
        y_b = _mla_attn(q, k, v, batch, seq, tq_attn, tk_attn)
        y_c = _conformer_conv(proj, conv_w, conv_b3, ln_g3, ln_b3, l, batch, seq, ts_conv)
        y_d = _sb_attn(proj, batch, seq, tq_attn, tk_attn)
        x2d = _merge(x2d, proj, y_a, y_b, y_c, y_d, w_branch_b, w_out_b, l, tm_merge)
        x2d = _ffn(x2d, g_ffn3, w_up_b, ffn_cw, ffn_b3, w_down_b, l, seq, tm_ffn, TK_FFN)
    return x2d.reshape(batch, seq, d)
```

```python
import functools

import jax
import jax.numpy as jnp
import numpy as np
from jax import lax
from jax.experimental import pallas as pl
from jax.experimental.pallas import tpu as pltpu

F32 = jnp.float32
BF16 = jnp.bfloat16

D_MODEL = 1024
HG_HEADS = 4
HG_DK = 64
HG_DV = 64
MLA_HEADS = 4
MLA_Q_RANK = 256
MLA_KV_RANK = 128
MLA_NOPE = 64
MLA_ROPE = 32
MLA_V = 64
MLA_QK = MLA_NOPE + MLA_ROPE
ROPE_BASE = 10000.0
CONV_CH = 256
CONV_K = 31
SB_HEADS = 4
SB_HEAD_DIM = 64
N_BRANCH = 4
BRANCH_W = 256
D_FF = 2816
FFN_CONV_K = 3
EPS = 1e-6
LN_EPS = 1e-5
LOG2E = 1.4426950408889634

IN_COLS = (256, 256, 256, 256, MLA_Q_RANK, MLA_KV_RANK, MLA_ROPE, 2 * CONV_CH, 256, 256, 256,
           N_BRANCH * D_MODEL)

LANES = 128
BF16_ROWS = 16
SUBLANES_F32 = 8
VMEM_LIMIT = 56 * 1024 * 1024

COL_GATE = 0
COL_HG = 4096
COL_MLA = 5120
COL_CU = 5632
COL_SB = 6144
N_IN = 6912
MLA_W = 512
TN_PROJ = 2304
TK_FFN = 256
HEAD_PAD = 128

HG_CHUNK = 64
HG_SUB = 16
HG_UNROLL = 4
HG_W = HG_HEADS * HG_DK


def _cparams(sem):
    return pltpu.CompilerParams(dimension_semantics=sem, vmem_limit_bytes=VMEM_LIMIT)


def _sigmoid(x):
    return 1.0 / (1.0 + jnp.exp(-x))


def _neg_abs(x):
    bits = pltpu.bitcast(x, jnp.uint32) | jnp.uint32(0x80000000)
    return pltpu.bitcast(bits, F32)


def _split_bf16(x):
    hi = x.astype(BF16)
    lo = (x - hi.astype(F32)).astype(BF16)
    return hi, lo


def _dot(a, b):
    return jnp.dot(a, b, preferred_element_type=F32)


def _dot_nt(a, b):
    return lax.dot_general(a, b, (((1,), (1,)), ((), ())), preferred_element_type=F32)


def _dot_tn(a, b):
    return lax.dot_general(a, b, (((0,), (0,)), ((), ())), preferred_element_type=F32)


def _norm_matmul_kernel(x_ref, g_ref, w_ref, o_ref, h_ref):
    @pl.when(pl.program_id(1) == 0)
    def _():
        x = x_ref[...]
        ms = jnp.mean(x * x, axis=-1, keepdims=True)
        h_ref[...] = (x * lax.rsqrt(ms + EPS) * g_ref[...]).astype(BF16)

    o_ref[...] = _dot(h_ref[...], w_ref[...]).astype(o_ref.dtype)


def _layer_spec(a, l):
    nd = a.ndim - 1
    return pl.BlockSpec((None,) + a.shape[1:], lambda *_: (l,) + (0,) * nd)


def _norm_matmul(x2d, g, w, l, tm, tn):
    t, d = x2d.shape
    n = w.shape[2]
    return pl.pallas_call(
        _norm_matmul_kernel,
        grid=(t // tm, n // tn),
        in_specs=[pl.BlockSpec((tm, d), lambda i, j: (i, 0)),
                  _layer_spec(g, l),
                  pl.BlockSpec((None, d, tn), lambda i, j: (l, 0, j))],
        out_specs=pl.BlockSpec((tm, tn), lambda i, j: (i, j)),
        out_shape=jax.ShapeDtypeStruct((t, n), BF16),
        scratch_shapes=[pltpu.VMEM((tm, d), BF16)],
        compiler_params=_cparams(("parallel", "arbitrary")),
        name="norm_matmul",
    )(x2d, g, w)


def _hgrn_kernel(p_ref, lbl_ref, lmask_ref, gout_ref, o_ref, st_ref, *, n_chunks):
    @pl.when(pl.program_id(1) == 0)
    def _():
        st_ref[...] = jnp.zeros_like(st_ref)

    c = HG_CHUNK
    w = HG_W
    logits = lbl_ref[...]
    mx = jnp.max(logits, axis=0, keepdims=True)
    ex = jnp.exp(logits - mx)
    sm = ex / jnp.sum(ex, axis=0, keepdims=True)
    lb = jnp.sum(sm * lmask_ref[...], axis=0, keepdims=True)
    log_lb = jnp.log(lb)
    log_1mlb = jnp.log1p(-lb)
    one_mlb = 1.0 - lb
    gout = gout_ref[...]

    row = lax.broadcasted_iota(jnp.int32, (c, c), 0)
    col = lax.broadcasted_iota(jnp.int32, (c, c), 1)
    tri = (col <= row).astype(BF16)
    rmod = lax.broadcasted_iota(jnp.int32, (c, w), 0) % HG_SUB
    rmod8 = lax.broadcasted_iota(jnp.int32, (c, w), 0) % SUBLANES_F32
    blk_t = lax.broadcasted_iota(jnp.int32, (HG_HEADS * c, c), 0) % c // HG_SUB
    blk_s = lax.broadcasted_iota(jnp.int32, (HG_HEADS * c, c), 1) // HG_SUB
    same_block = blk_t == blk_s
    lane_head = lax.broadcasted_iota(jnp.int32, (c, w), 1) // HG_DK
    eh_r = lax.broadcasted_iota(jnp.int32, (w, w), 0) // HG_DK
    eh_c = lax.broadcasted_iota(jnp.int32, (w, w), 1) // HG_DK
    same_head = eh_r == eh_c
    eh = same_head.astype(BF16)
    n_sub = c // HG_SUB
    cat_head = lax.broadcasted_iota(jnp.int32, (c, (n_sub - 1) * w), 1) % w // HG_DK

    def chunk(ci, carry):
        r0 = pl.multiple_of(ci * c, c)
        blk = p_ref[pl.ds(r0, c), :].astype(F32)
        hq, hf, hv, hg = blk[:, :w], blk[:, w:2 * w], blk[:, 2 * w:3 * w], blk[:, 3 * w:]
        q = hq * _sigmoid(hq)
        e = jnp.exp(-jnp.abs(hf))
        l1pe = jnp.log(1.0 + e)
        logsig = jnp.minimum(hf, 0.0) - l1pe
        b_term = log_1mlb + logsig
        mxab = jnp.maximum(log_lb, b_term)
        lf = mxab + jnp.log(1.0 + jnp.exp(-jnp.abs(log_lb - b_term)))
        kk = one_mlb * jnp.where(hf >= 0.0, e, 1.0) / (1.0 + e)
        v_bf = hv.astype(BF16)

        lf_hi, lf_lo = _split_bf16(lf)
        bcum = _dot(tri, lf_hi) + _dot(tri, lf_lo)
        b_last = bcum[c - 1:c, :]

        st = st_ref[...]
        qe = (q * jnp.exp(bcum)).astype(BF16)
        o = _dot_nt(qe, st.astype(BF16))

        qms, kms = [], []
        for i in range(1, n_sub):
            lo_r, hi_r = i * HG_SUB, (i + 1) * HG_SUB
            ref_row = bcum[lo_r - 1:lo_r, :]
            qi = q[lo_r:hi_r] * jnp.exp(bcum[lo_r:hi_r] - ref_row)
            pieces = [jnp.zeros((lo_r, w), F32), qi]
            if hi_r < c:
                pieces.append(jnp.zeros((c - hi_r, w), F32))
            qms.append(jnp.concatenate(pieces, axis=0))
            ki = kk[:lo_r] * jnp.exp(ref_row - bcum[:lo_r])
            kms.append(jnp.concatenate([ki, jnp.zeros((c - lo_r, w), F32)], axis=0))
        qc = jnp.concatenate(qms, axis=1)
        kc = jnp.concatenate(kms, axis=1).astype(BF16)
        qstack = jnp.concatenate(
            [jnp.where(cat_head == h, qc, 0.0) for h in range(HG_HEADS)], axis=0).astype(BF16)
        a_stack = _dot_nt(qstack, kc)

        half = HG_SUB // 2
        ref2 = jnp.concatenate(
            [jnp.broadcast_to(bcum[i * HG_SUB + half - 1:i * HG_SUB + half, :], (HG_SUB, w))
             for i in range(n_sub)], axis=0)
        upper = rmod >= half
        q2 = jnp.where(upper, q * jnp.exp(jnp.where(upper, bcum - ref2, 0.0)), 0.0)
        k2 = jnp.where(upper, 0.0, kk * jnp.exp(jnp.where(upper, 0.0, ref2 - bcum))).astype(BF16)
        q2stack = jnp.concatenate(
            [jnp.where(lane_head == h, q2, 0.0) for h in range(HG_HEADS)], axis=0).astype(BF16)
        a2 = _dot_nt(q2stack, k2)
        a_tot = a_stack + jnp.where(same_block, a2, 0.0)
        r_full = _dot(a_tot.astype(BF16), v_bf)
        for h in range(HG_HEADS):
            o = o + jnp.where(lane_head == h, r_full[h * c:(h + 1) * c], 0.0)

        def shift(a, d):
            a3 = a.reshape(c // SUBLANES_F32, SUBLANES_F32, w)
            return pltpu.roll(a3, d, 1).reshape(c, w)

        for d in range(half):
            if d == 0:
                ks, bs, vs = kk, bcum, hv
            else:
                ks, bs, vs = shift(kk, d), shift(bcum, d), shift(hv, d)
            valid = rmod8 >= d
            pd = jnp.where(valid, q * ks * jnp.exp(jnp.where(valid, bcum - bs, 0.0)), 0.0)
            o = o + _dot(pd.astype(BF16), eh) * vs

        kdec = (kk * jnp.exp(b_last - bcum)).astype(BF16)
        upd = _dot_tn(v_bf, kdec)
        st_ref[...] = st * jnp.exp(b_last) + jnp.where(same_head, upd, 0.0)

        o2_hi, o2_lo = _split_bf16(o * o)
        ms = (_dot(o2_hi, eh) + _dot(o2_lo, eh)) * (1.0 / HG_DV)
        y = o * lax.rsqrt(ms + EPS) * gout * (hg * _sigmoid(hg))
        o_ref[pl.ds(r0, c), :] = y.astype(o_ref.dtype)
        return carry

    def trip(ti, carry):
        for u in range(HG_UNROLL):
            carry = chunk(ti * HG_UNROLL + u, carry)
        return carry

    lax.fori_loop(0, n_chunks // HG_UNROLL, trip, 0)


def _hgrn(proj, lb_logits, lmasks, gout_t, l, batch, seq, ts):
    t = proj.shape[0]
    nsb = seq // ts
    cb = COL_HG // (4 * HG_W)
    return pl.pallas_call(
        functools.partial(_hgrn_kernel, n_chunks=ts // HG_CHUNK),
        grid=(batch, nsb),
        in_specs=[pl.BlockSpec((ts, 4 * HG_W), lambda b, s: (b * nsb + s, cb)),
                  pl.BlockSpec(lb_logits.shape, lambda b, s: (0, 0)),
                  _layer_spec(lmasks, l),
                  _layer_spec(gout_t, l)],
        out_specs=pl.BlockSpec((ts, HG_W), lambda b, s: (b * nsb + s, 0)),
        out_shape=jax.ShapeDtypeStruct((t, HG_W), BF16),
        scratch_shapes=[pltpu.VMEM((HG_W, HG_W), F32)],
        compiler_params=_cparams(("parallel", "arbitrary")),
        name="hgrn2",
    )(proj, lb_logits, lmasks, gout_t)


def _mla_prep_kernel(p_ref, gq_ref, wuq_ref, gkv_ref, wkc_ref, wuv_ref, gqq_ref, gqk_ref,
                     cos_ref, sin_ref, q_ref, k_ref, v_ref):
    qk_w = MLA_HEADS * HEAD_PAD
    blk = p_ref[...].astype(F32)
    cq = blk[:, :MLA_Q_RANK]
    ckv = blk[:, MLA_Q_RANK:MLA_Q_RANK + MLA_KV_RANK]
    krp = blk[:, MLA_Q_RANK + MLA_KV_RANK:]

    def rms(x, g):
        ms = jnp.mean(x * x, axis=-1, keepdims=True)
        return x * lax.rsqrt(ms + EPS) * g

    cqn = rms(cq, gq_ref[...]).astype(BF16)
    ckvn = rms(ckv, gkv_ref[...]).astype(BF16)
    q_raw = _dot(cqn, wuq_ref[...])
    k_raw = _dot(jnp.concatenate([ckvn, krp.astype(BF16)], axis=1), wkc_ref[...])
    v_ref[...] = _dot(ckvn, wuv_ref[...]).astype(v_ref.dtype)

    cos, sin = cos_ref[...], sin_ref[...]

    def head(raw, h, g_ref, scale):
        x = raw[:, h * HEAD_PAD:(h + 1) * HEAD_PAD]
        xs = raw[:, qk_w + h * HEAD_PAD:qk_w + (h + 1) * HEAD_PAD]
        ms = jnp.sum(x * x, axis=-1, keepdims=True) * (1.0 / MLA_QK)
        r = lax.rsqrt(ms + EPS) * scale
        return (x * g_ref[:, :HEAD_PAD] * cos + xs * g_ref[:, HEAD_PAD:] * sin) * r

    for h in range(MLA_HEADS):
        sl = slice(h * HEAD_PAD, (h + 1) * HEAD_PAD)
        q_ref[:, sl] = head(q_raw, h, gqq_ref, MLA_QK ** -0.5 * LOG2E).astype(q_ref.dtype)
        k_ref[:, sl] = head(k_raw, h, gqk_ref, 1.0).astype(k_ref.dtype)


def _mla_prep(proj, gq, wuq, gkv, wkc, wuv, gqq, gqk, cos_t, sin_t, l, seq, tm):
    t = proj.shape[0]
    nsb = seq // tm
    cb = COL_MLA // MLA_W
    full = lambda a: _layer_spec(a, l)
    tab = pl.BlockSpec((tm, HEAD_PAD), lambda i: (i % nsb, 0))
    qk_w = MLA_HEADS * HEAD_PAD
    return pl.pallas_call(
        _mla_prep_kernel,
        grid=(t // tm,),
        in_specs=[pl.BlockSpec((tm, MLA_W), lambda i: (i, cb)),
                  full(gq), full(wuq), full(gkv), full(wkc), full(wuv), full(gqq), full(gqk),
                  tab, tab],
        out_specs=[pl.BlockSpec((tm, qk_w), lambda i: (i, 0)),
                   pl.BlockSpec((tm, qk_w), lambda i: (i, 0)),
                   pl.BlockSpec((tm, MLA_HEADS * MLA_V), lambda i: (i, 0))],
        out_shape=[jax.ShapeDtypeStruct((t, qk_w), BF16),
                   jax.ShapeDtypeStruct((t, qk_w), BF16),
                   jax.ShapeDtypeStruct((t, MLA_HEADS * MLA_V), BF16)],
        compiler_params=_cparams(("parallel",)),
        name="mla_prep",
    )(proj, gq, wuq, gkv, wkc, wuv, gqq, gqk, cos_t, sin_t)


def _mla_attn_kernel(q_ref, k_ref, v_ref, o_ref, *, tq, tk):
    i = pl.program_id(2)
    n_diag = tq // tk
    lane = lax.broadcasted_iota(jnp.int32, (tq, LANES), 1)
    qs = [q_ref[:, hh * HEAD_PAD:(hh + 1) * HEAD_PAD] for hh in range(2)]

    row = lax.broadcasted_iota(jnp.int32, (tq, tk), 0)
    col = lax.broadcasted_iota(jnp.int32, (tq, tk), 1)

    def step(kc, carry, dj):
        k0 = pl.multiple_of(kc * tk, tk)
        vb = v_ref[pl.ds(k0, tk), :]
        out = []
        for hh in range(2):
            m, l, acc = carry[hh]
            kb = k_ref[pl.ds(k0, tk), hh * HEAD_PAD:(hh + 1) * HEAD_PAD]
            s = _dot_nt(qs[hh], kb)
            if dj is not None:
                s = jnp.where(col + dj * tk <= row, s, -jnp.inf)
            m_new = jnp.maximum(m, jnp.max(s, axis=-1, keepdims=True))
            alpha = jnp.exp2(m - m_new)
            p = jnp.exp2(s - m_new)
            l = alpha * l + jnp.sum(p, axis=-1, keepdims=True)
            acc = alpha * acc + _dot(p.astype(BF16), vb)
            out.append((m_new, l, acc))
        return tuple(out)

    one = (jnp.full((tq, 1), -jnp.inf, F32), jnp.zeros((tq, 1), F32), jnp.zeros((tq, LANES), F32))
    carry = step(i * n_diag, (one, one), 0)
    for dj in range(1, n_diag):
        carry = step(i * n_diag + dj, carry, dj)
    carry = lax.fori_loop(0, i * n_diag, lambda kc, cr: step(kc, cr, None), carry)
    outs = [acc / l for (_, l, acc) in carry]
    o_ref[...] = jnp.where(lane < MLA_V, outs[0], outs[1]).astype(o_ref.dtype)


def _mla_attn(q, k, v, batch, seq, tq, tk):
    t = q.shape[0]
    nqb = seq // tq
    return pl.pallas_call(
        functools.partial(_mla_attn_kernel, tq=tq, tk=tk),
        grid=(batch, MLA_HEADS // 2, nqb),
        in_specs=[pl.BlockSpec((tq, 2 * HEAD_PAD), lambda b, p, i: (b * nqb + i, p)),
                  pl.BlockSpec((seq, 2 * HEAD_PAD), lambda b, p, i: (b, p)),
                  pl.BlockSpec((seq, LANES), lambda b, p, i: (b, p))],
        out_specs=pl.BlockSpec((tq, LANES), lambda b, p, i: (b * nqb + i, p)),
        out_shape=jax.ShapeDtypeStruct((t, MLA_HEADS * MLA_V), BF16),
        compiler_params=_cparams(("parallel", "parallel", "arbitrary")),
        name="mla_attn",
    )(q, k, v)


CONV_HALO = 32
CONV_ROWS = 64


SUBLANES = SUBLANES_F32


def _conv_kernel(p_ref, w_ref, b_ref, g_ref, beta_ref, o_ref, halo_ref, v_ref, *, ts):
    @pl.when(pl.program_id(1) == 0)
    def _():
        halo_ref[...] = jnp.zeros_like(halo_ref)

    blk = p_ref[...].astype(F32)
    u = blk[:, :CONV_CH] * _sigmoid(blk[:, CONV_CH:])
    u_ext = jnp.concatenate([halo_ref[...], u], axis=0)
    halo_ref[...] = u[ts - CONV_HALO:]
    n_ext = CONV_HALO + ts
    v_ref[0] = u_ext
    for m in range(1, SUBLANES):
        v_ref[m] = pltpu.roll(u_ext, n_ext - m, 0)
    w = w_ref[...]
    bias = b_ref[...]
    gam, beta = g_ref[...], beta_ref[...]
    first = CONV_HALO - (CONV_K - 1)
    for r in range(ts // CONV_ROWS):
        acc = jnp.zeros((CONV_ROWS, CONV_CH), F32) + bias
        for j in range(CONV_K):
            off = first + j
            start = r * CONV_ROWS + (off // SUBLANES) * SUBLANES
            acc = acc + w[j:j + 1, :] * v_ref[off % SUBLANES, start:start + CONV_ROWS, :]
        mu = jnp.mean(acc, axis=-1, keepdims=True)
        xc = acc - mu
        var = jnp.mean(xc * xc, axis=-1, keepdims=True)
        y = xc * lax.rsqrt(var + LN_EPS) * gam + beta
        o_ref[r * CONV_ROWS:(r + 1) * CONV_ROWS, :] = (y * _sigmoid(y)).astype(o_ref.dtype)


def _conformer_conv(proj, w, b, g, beta, l, batch, seq, ts):
    t = proj.shape[0]
    nsb = seq // ts
    cb = COL_CU // (2 * CONV_CH)
    full = lambda a: _layer_spec(a, l)
    return pl.pallas_call(
        functools.partial(_conv_kernel, ts=ts),
        grid=(batch, nsb),
        in_specs=[pl.BlockSpec((ts, 2 * CONV_CH), lambda bb, s: (bb * nsb + s, cb)),
                  full(w), full(b), full(g), full(beta)],
        out_specs=pl.BlockSpec((ts, CONV_CH), lambda bb, s: (bb * nsb + s, 0)),
        out_shape=jax.ShapeDtypeStruct((t, CONV_CH), BF16),
        scratch_shapes=[pltpu.VMEM((CONV_HALO, CONV_CH), F32),
                        pltpu.VMEM((SUBLANES, CONV_HALO + ts, CONV_CH), F32)],
        compiler_params=_cparams(("parallel", "arbitrary")),
        name="conformer_conv",
    )(proj, w, b, g, beta)


SB_SUB = 256


def _sb_attn_kernel(q_ref, k_ref, v_ref, o_ref, *, tq, tk):
    i = pl.program_id(2)
    n_diag = tq // tk
    n_sub = tk // SB_SUB
    lane = lax.broadcasted_iota(jnp.int32, (tq, LANES), 1)
    r2 = lax.broadcasted_iota(jnp.int32, (SB_SUB, SB_SUB), 0)
    c2 = lax.broadcasted_iota(jnp.int32, (SB_SUB, SB_SUB), 1)
    suffix = (r2 >= c2).astype(BF16)
    q = q_ref[...]
    qs = [jnp.where((lane >= hh * SB_HEAD_DIM) & (lane < (hh + 1) * SB_HEAD_DIM), q,
                    jnp.zeros_like(q)) for hh in range(2)]

    def softplus2(z):
        return jnp.maximum(z, 0.0) + jnp.log(1.0 + jnp.exp2(_neg_abs(z))) * LOG2E

    def step(kc, carry):
        k0 = pl.multiple_of(kc * tk, tk)
        kb = k_ref[pl.ds(k0, tk), :]
        vb = v_ref[pl.ds(k0, tk), :]
        out = []
        for hh in range(2):
            run, acc = carry[hh]
            z_all = _dot_nt(qs[hh], kb)
            wgts = [None] * n_sub
            for sub in reversed(range(n_sub)):
                z = z_all[:, sub * SB_SUB:(sub + 1) * SB_SUB]
                incl = _dot(softplus2(z).astype(BF16), suffix)
                wgts[sub] = jnp.exp2(z - incl - run).astype(BF16)
                run = run + incl[:, :1]
            wcat = wgts[0] if n_sub == 1 else jnp.concatenate(wgts, axis=1)
            out.append((run, acc + _dot(wcat, vb)))
        return tuple(out)

    def diag_step(kc, carry, dj):
        k0 = pl.multiple_of(kc * tk, tk)
        out = []
        for hh in range(2):
            run, acc = carry[hh]
            for sub in reversed(range(n_sub)):
                r0 = dj * tk + sub * SB_SUB
                if r0 >= tq:
                    continue
                nr = tq - r0
                kb = k_ref[pl.ds(k0 + sub * SB_SUB, SB_SUB), :]
                vb = v_ref[pl.ds(k0 + sub * SB_SUB, SB_SUB), :]
                z = _dot_nt(qs[hh][r0:], kb)
                valid = (lax.broadcasted_iota(jnp.int32, (nr, SB_SUB), 1)
                         < lax.broadcasted_iota(jnp.int32, (nr, SB_SUB), 0))
                incl = _dot(jnp.where(valid, softplus2(z), 0.0).astype(BF16), suffix)
                wgt = jnp.where(valid, jnp.exp2(z - incl - run[r0:]), 0.0).astype(BF16)
                run_lo = run[r0:] + incl[:, :1]
                acc_lo = acc[r0:] + _dot(wgt, vb)
                if r0 > 0:
                    run = jnp.concatenate([run[:r0], run_lo], axis=0)
                    acc = jnp.concatenate([acc[:r0], acc_lo], axis=0)
                else:
                    run, acc = run_lo, acc_lo
            out.append((run, acc))
        return tuple(out)

    one = (jnp.zeros((tq, 1), F32), jnp.zeros((tq, LANES), F32))
    carry = (one, one)
    for dj in reversed(range(n_diag)):
        carry = diag_step(i * n_diag + dj, carry, dj)
    nfull = i * n_diag
    carry = lax.fori_loop(0, nfull, lambda it, cr: step(nfull - 1 - it, cr), carry)
    o_ref[...] = jnp.where(lane < SB_HEAD_DIM, carry[0][1], carry[1][1]).astype(o_ref.dtype)


def _sb_attn(proj, batch, seq, tq, tk):
    t = proj.shape[0]
    nqb = seq // tq
    qc, kc, vc = (COL_SB // LANES, (COL_SB + 256) // LANES, (COL_SB + 512) // LANES)
    return pl.pallas_call(
        functools.partial(_sb_attn_kernel, tq=tq, tk=tk),
        grid=(batch, SB_HEADS // 2, nqb),
        in_specs=[pl.BlockSpec((tq, LANES), lambda b, p, i: (b * nqb + i, qc + p)),
                  pl.BlockSpec((seq, LANES), lambda b, p, i: (b, kc + p)),
                  pl.BlockSpec((seq, LANES), lambda b, p, i: (b, vc + p))],
        out_specs=pl.BlockSpec((tq, LANES), lambda b, p, i: (b * nqb + i, p)),
        out_shape=jax.ShapeDtypeStruct((t, SB_HEADS * SB_HEAD_DIM), BF16),
        compiler_params=_cparams(("parallel", "parallel", "arbitrary")),
        name="sb_attn",
    )(proj, proj, proj)


def _merge_kernel(x_ref, gl_ref, ya_ref, yb_ref, yc_ref, yd_ref, wb_ref, wo_ref, o_ref):
    merged = None
    for n, y_ref in enumerate((ya_ref, yb_ref, yc_ref, yd_ref)):
        hb = _dot(y_ref[...], wb_ref[n])
        th = jnp.tanh(gl_ref[:, n * D_MODEL:(n + 1) * D_MODEL].astype(F32))
        term = hb + hb * th
        merged = term if merged is None else merged + term
    o_ref[...] = x_ref[...] + _dot(merged.astype(BF16), wo_ref[...])


def _merge(x2d, proj, ya, yb, yc, yd, wb, wo, l, tm):
    t, d = x2d.shape
    yspec = pl.BlockSpec((tm, BRANCH_W), lambda i: (i, 0))
    return pl.pallas_call(
        _merge_kernel,
        grid=(t // tm,),
        in_specs=[pl.BlockSpec((tm, d), lambda i: (i, 0)),
                  pl.BlockSpec((tm, N_BRANCH * d), lambda i: (i, COL_GATE // (N_BRANCH * d))),
                  yspec, yspec, yspec, yspec,
                  _layer_spec(wb, l), _layer_spec(wo, l)],
        out_specs=pl.BlockSpec((tm, d), lambda i: (i, 0)),
        out_shape=jax.ShapeDtypeStruct((t, d), F32),
        compiler_params=_cparams(("parallel",)),
        name="merge_out",
    )(x2d, proj, ya, yb, yc, yd, wb, wo)


FFN_HALO = 16
FFN_ROW_SPLIT = 1


def _ffn_kernel(x_ref, xh_ref, g_ref, wv_ref, wg_ref, cwv_ref, cwg_ref, cbv_ref, cbg_ref,
                wd_ref, o_ref, h_ref, acc_ref, act_a, act_b, *, nk, n_row_blocks, blocks_per_seq):
    s = pl.program_id(0)
    c = s % nk
    i = jnp.minimum(s // nk, n_row_blocks - 1)

    def norm(x):
        ms = jnp.mean(x * x, axis=-1, keepdims=True)
        return (x * lax.rsqrt(ms + EPS) * g_ref[...]).astype(BF16)

    @pl.when(s == 0)
    def _():
        act_a[...] = jnp.zeros_like(act_a)
        acc_ref[...] = jnp.zeros_like(acc_ref)

    @pl.when(c == 0)
    def _():
        h_ref[FFN_HALO:, :] = norm(x_ref[...])
        keep = (i % blocks_per_seq != 0).astype(F32)
        h_ref[0:FFN_HALO, :] = norm(xh_ref[...] * keep)

    def conv(ext, cw_ref, cb_ref):
        cw = cw_ref[...]
        u1 = pltpu.roll(ext, 1, 0)[FFN_HALO:]
        u2 = pltpu.roll(ext, 2, 0)[FFN_HALO:]
        return cw[2:3] * ext[FFN_HALO:] + cw[1:2] * u1 + cw[0:1] * u2 + cb_ref[...]

    tm = acc_ref.shape[0]
    rs = tm // FFN_ROW_SPLIT

    def main(act_in, act_out):
        tails = None
        for j in range(FFN_ROW_SPLIT):
            lo = FFN_HALO + j * rs
            if j == 0:
                hs = h_ref[0:lo + rs, :]
                exts = [_dot(hs, w_ref[...]) for w_ref in (wv_ref, wg_ref)]
            else:
                hs = h_ref[lo:lo + rs, :]
                exts = [jnp.concatenate([t, _dot(hs, w_ref[...])], axis=0)
                        for t, w_ref in zip(tails, (wv_ref, wg_ref))]
            tails = [e[rs:] for e in exts]
            val = conv(exts[0], cwv_ref, cbv_ref)
            hg = conv(exts[1], cwg_ref, cbg_ref)
            act_out[j * rs:(j + 1) * rs, :] = ((hg + hg * jnp.tanh(hg)) * val).astype(BF16)
        acc_ref[...] += _dot(act_in[...], wd_ref[...])

    @pl.when(s % 2 == 0)
    def _():
        main(act_a, act_b)

    @pl.when(s % 2 == 1)
    def _():
        main(act_b, act_a)

    @pl.when(c == 0)
    def _():
        o_ref[...] = acc_ref[...]
        acc_ref[...] = x_ref[...]


def _ffn(x2d, g, w_up, cw, cb, w_down, l, seq, tm, tk):
    t, d = x2d.shape
    nk = D_FF // tk
    hb = tm // FFN_HALO
    nrb = t // tm
    row_blk = lambda s: jnp.minimum(s // nk, nrb - 1)
    return pl.pallas_call(
        functools.partial(_ffn_kernel, nk=nk, n_row_blocks=nrb, blocks_per_seq=seq // tm),
        grid=(nrb * nk + 1,),
        in_specs=[pl.BlockSpec((tm, d), lambda s: (row_blk(s), 0)),
                  pl.BlockSpec((FFN_HALO, d), lambda s: (jnp.maximum(row_blk(s) * hb - 1, 0), 0)),
                  _layer_spec(g, l),
                  pl.BlockSpec((None, d, tk), lambda s: (l, 0, s % nk)),
                  pl.BlockSpec((None, d, tk), lambda s: (l, 0, nk + s % nk)),
                  pl.BlockSpec((None, FFN_CONV_K, tk), lambda s: (l, 0, s % nk)),
                  pl.BlockSpec((None, FFN_CONV_K, tk), lambda s: (l, 0, nk + s % nk)),
                  pl.BlockSpec((None, 1, tk), lambda s: (l, 0, s % nk)),
                  pl.BlockSpec((None, 1, tk), lambda s: (l, 0, nk + s % nk)),
                  pl.BlockSpec((None, tk, d), lambda s: (l, (s + nk - 1) % nk, 0))],
        out_specs=pl.BlockSpec((tm, d), lambda s: (jnp.maximum(s - 1, 0) // nk, 0)),
        out_shape=jax.ShapeDtypeStruct((t, d), F32),
        scratch_shapes=[pltpu.VMEM((tm + FFN_HALO, d), BF16), pltpu.VMEM((tm, d), F32),
                        pltpu.VMEM((tm, tk), BF16), pltpu.VMEM((tm, tk), BF16)],
        compiler_params=_cparams(("arbitrary",)),
        name="ffn",
    )(x2d, x2d, g, w_up, w_up, cw, cw, cb, cb, w_down)


def _prep_w_in(w_in):
    offs = np.concatenate([[0], np.cumsum(IN_COLS)])
    seg = lambda i, c=1.0: (w_in[..., offs[i]:offs[i + 1]] * c).astype(BF16)
    pad = jnp.zeros(w_in.shape[:-1] + (MLA_W - MLA_Q_RANK - MLA_KV_RANK - MLA_ROPE,), BF16)
    parts = [seg(11, 0.5), seg(0), seg(1), seg(2), seg(3), seg(4), seg(5), seg(6), pad,
             seg(7), seg(8, SB_HEAD_DIM ** -0.5 * LOG2E), seg(9), seg(10)]
    return jnp.concatenate(parts, axis=-1)


def _prep_mla_weights(w_uq, w_ukv):
    nl = w_uq.shape[0]
    wq = w_uq.reshape(nl, MLA_Q_RANK, MLA_HEADS, MLA_QK)
    wq = jnp.pad(wq, ((0, 0), (0, 0), (0, 0), (0, HEAD_PAD - MLA_QK)))
    wq = wq.reshape(nl, MLA_Q_RANK, MLA_HEADS * HEAD_PAD)
    wkv = w_ukv.reshape(nl, MLA_KV_RANK, MLA_HEADS, MLA_NOPE + MLA_V)
    wk = jnp.pad(wkv[..., :MLA_NOPE], ((0, 0), (0, 0), (0, 0), (0, HEAD_PAD - MLA_NOPE)))
    wk = wk.reshape(nl, MLA_KV_RANK, MLA_HEADS * HEAD_PAD)
    wv = wkv[..., MLA_NOPE:].reshape(nl, MLA_KV_RANK, MLA_HEADS * MLA_V)
    place = np.zeros((LANES, MLA_HEADS * HEAD_PAD), np.float32)
    for h in range(MLA_HEADS):
        for r in range(MLA_ROPE):
            place[r, h * HEAD_PAD + MLA_NOPE + r] = 1.0
    wkc = jnp.concatenate([wk, jnp.broadcast_to(jnp.asarray(place), (nl,) + place.shape)], axis=1)
    wq = jnp.concatenate([wq, _swap_rope_lanes(wq)], axis=-1)
    wkc = jnp.concatenate([wkc, _swap_rope_lanes(wkc)], axis=-1)
    return wq.astype(BF16), wkc.astype(BF16), wv.astype(BF16)


def _swap_rope_lanes(a):
    half = MLA_ROPE // 2
    perm = np.arange(HEAD_PAD)
    perm[MLA_NOPE:MLA_NOPE + half] = np.arange(MLA_NOPE + half, MLA_QK)
    perm[MLA_NOPE + half:MLA_QK] = np.arange(MLA_NOPE, MLA_NOPE + half)
    groups = a.shape[-1] // HEAD_PAD
    idx = (np.arange(groups)[:, None] * HEAD_PAD + perm[None, :]).reshape(-1)
    return a[..., idx]


def _qk_gain(g):
    gp = jnp.pad(g, ((0, 0), (0, HEAD_PAD - MLA_QK)))
    return jnp.concatenate([gp, _swap_rope_lanes(gp)], axis=-1)[:, None, :]


def _rope_tables(seq):
    half = MLA_ROPE // 2
    freqs = ROPE_BASE ** (-jnp.arange(half, dtype=F32) / half)
    ang = jnp.arange(seq, dtype=F32)[:, None] * freqs[None, :]
    cos, sin = jnp.cos(ang), jnp.sin(ang)
    tail = HEAD_PAD - MLA_QK
    cos_t = jnp.concatenate([jnp.ones((seq, MLA_NOPE), F32), cos, cos, jnp.ones((seq, tail), F32)],
                            axis=1)
    sin_t = jnp.concatenate([jnp.zeros((seq, MLA_NOPE), F32), -sin, sin, jnp.zeros((seq, tail), F32)],
                            axis=1)
    return cos_t, sin_t


def kernel(x, g_mix, w_in, lb_logits, g_hg_out, g_q_lat, w_uq, g_kv_lat, w_ukv, g_qk_q, g_qk_k,
           conv_w, conv_b, conv_ln_g, conv_ln_b, w_branch, w_out, g_ffn, w_up, ffn_conv_w,
           ffn_conv_b, w_down):
    batch, seq, d = x.shape
    depth = w_in.shape[0]
    t = batch * seq

    tm_proj = min(1024, seq)
    ts_hg = min(512, seq)
    tm_mla = min(512, seq)
    tq_attn = min(1024, seq)
    tk_attn = min(1024, seq)
    ts_conv = min(512, seq)
    tm_merge = min(512, seq)
    tm_ffn = min(1024, seq)

    w_in_p = _prep_w_in(w_in)
    wq_p, wkc_p, wv_p = _prep_mla_weights(w_uq, w_ukv)
    cos_t, sin_t = _rope_tables(seq)
    rows = lambda a: a[:, None, :]
    gqq_p = _qk_gain(g_qk_q)
    gqk_p = _qk_gain(g_qk_k)
    gout_t = rows(jnp.tile(g_hg_out, (1, HG_HEADS)))
    w_branch_b = (0.5 * w_branch).astype(BF16)
    w_out_b = w_out.astype(BF16)
    w_up_b = w_up.astype(BF16)
    w_down_b = w_down.astype(BF16)
    ids = np.arange(depth)
    lmasks = jnp.asarray(((ids[None, :] >= 1) & (ids[None, :] <= ids[:, None]))
                         .astype(np.float32)[:, :, None])
    lb_logits = lb_logits.astype(F32)
    g_mix3, g_ffn3, g_q3, g_kv3 = rows(g_mix), rows(g_ffn), rows(g_q_lat), rows(g_kv_lat)
    conv_b3, ln_g3, ln_b3 = rows(conv_b), rows(conv_ln_g), rows(conv_ln_b)
    gate_half = jnp.asarray(np.concatenate([np.ones(D_FF, np.float32), np.full(D_FF, 0.5, np.float32)]))
    ffn_cw = ffn_conv_w * gate_half
    ffn_b3 = rows(ffn_conv_b * gate_half)

    x2d = x.reshape(t, d)
    for l in range(depth):
        proj = _norm_matmul(x2d, g_mix3, w_in_p, l, tm_proj, TN_PROJ)
        y_a = _hgrn(proj, lb_logits, lmasks, gout_t, l, batch, seq, ts_hg)
        q, k, v = _mla_prep(proj, g_q3, wq_p, g_kv3, wkc_p, wv_p, gqq_p, gqk_p,
                            cos_t, sin_t, l, seq, tm_mla)
        y_b = _mla_attn(q, k, v, batch, seq, tq_attn, tk_attn)
        y_c = _conformer_conv(proj, conv_w, conv_b3, ln_g3, ln_b3, l, batch, seq, ts_conv)
        y_d = _sb_attn(proj, batch, seq, tq_attn, tk_attn)
        x2d = _merge(x2d, proj, y_a, y_b, y_c, y_d, w_branch_b, w_out_b, l, tm_merge)
        x2d = _ffn(x2d, g_ffn3, w_up_b, ffn_cw, ffn_b3, w_down_b, l, seq, tm_ffn, TK_FFN)
    return x2d.reshape(batch, seq, d)
```

```python
import functools

import jax
import jax.numpy as jnp
import numpy as np
from jax import lax
from jax.experimental import pallas as pl
from jax.experimental.pallas import tpu as pltpu

F32 = jnp.float32
BF16 = jnp.bfloat16

D_MODEL = 1024
HG_HEADS = 4
HG_DK = 64
HG_DV = 64
MLA_HEADS = 4
MLA_Q_RANK = 256
MLA_KV_RANK = 128
MLA_NOPE = 64
MLA_ROPE = 32
MLA_V = 64
MLA_QK = MLA_NOPE + MLA_ROPE
ROPE_BASE = 10000.0
CONV_CH = 256
CONV_K = 31
SB_HEADS = 4
SB_HEAD_DIM = 64
N_BRANCH = 4
BRANCH_W = 256
D_FF = 2816
FFN_CONV_K = 3
EPS = 1e-6
LN_EPS = 1e-5
LOG2E = 1.4426950408889634

IN_COLS = (256, 256, 256, 256, MLA_Q_RANK, MLA_KV_RANK, MLA_ROPE, 2 * CONV_CH, 256, 256, 256,
           N_BRANCH * D_MODEL)

LANES = 128
BF16_ROWS = 16
SUBLANES_F32 = 8
VMEM_LIMIT = 56 * 1024 * 1024

COL_GATE = 0
COL_HG = 4096
COL_MLA = 5120
COL_CU = 5632
COL_SB = 6144
N_IN = 6912
MLA_W = 512
TN_PROJ = 2304
TK_FFN = 256
HEAD_PAD = 128

HG_CHUNK = 64
HG_SUB = 16
HG_UNROLL = 4
HG_W = HG_HEADS * HG_DK


def _cparams(sem):
    return pltpu.CompilerParams(dimension_semantics=sem, vmem_limit_bytes=VMEM_LIMIT)


def _sigmoid(x):
    return 1.0 / (1.0 + jnp.exp(-x))


def _neg_abs(x):
    bits = pltpu.bitcast(x, jnp.uint32) | jnp.uint32(0x80000000)
    return pltpu.bitcast(bits, F32)


def _split_bf16(x):
    hi = x.astype(BF16)
    lo = (x - hi.astype(F32)).astype(BF16)
    return hi, lo


def _dot(a, b):
    return jnp.dot(a, b, preferred_element_type=F32)


def _dot_nt(a, b):
    return lax.dot_general(a, b, (((1,), (1,)), ((), ())), preferred_element_type=F32)


def _dot_tn(a, b):
    return lax.dot_general(a, b, (((0,), (0,)), ((), ())), preferred_element_type=F32)


def _norm_matmul_kernel(x_ref, g_ref, w_ref, o_ref, h_ref):
    @pl.when(pl.program_id(1) == 0)
    def _():
        x = x_ref[...]
        ms = jnp.mean(x * x, axis=-1, keepdims=True)
        h_ref[...] = (x * lax.rsqrt(ms + EPS) * g_ref[...]).astype(BF16)

    o_ref[...] = _dot(h_ref[...], w_ref[...]).astype(o_ref.dtype)


def _layer_spec(a, l):
    nd = a.ndim - 1
    return pl.BlockSpec((None,) + a.shape[1:], lambda *_: (l,) + (0,) * nd)


def _norm_matmul(x2d, g, w, l, tm, tn):
    t, d = x2d.shape
    n = w.shape[2]
    return pl.pallas_call(
        _norm_matmul_kernel,
        grid=(t // tm, n // tn),
        in_specs=[pl.BlockSpec((tm, d), lambda i, j: (i, 0)),
                  _layer_spec(g, l),
                  pl.BlockSpec((None, d, tn), lambda i, j: (l, 0, j))],
        out_specs=pl.BlockSpec((tm, tn), lambda i, j: (i, j)),
        out_shape=jax.ShapeDtypeStruct((t, n), BF16),
        scratch_shapes=[pltpu.VMEM((tm, d), BF16)],
        compiler_params=_cparams(("parallel", "arbitrary")),
        name="norm_matmul",
    )(x2d, g, w)


def _hgrn_kernel(p_ref, lbl_ref, lmask_ref, gout_ref, o_ref, st_ref, *, n_chunks):
    @pl.when(pl.program_id(1) == 0)
    def _():
        st_ref[...] = jnp.zeros_like(st_ref)

    c = HG_CHUNK
    w = HG_W
    logits = lbl_ref[...]
    mx = jnp.max(logits, axis=0, keepdims=True)
    ex = jnp.exp(logits - mx)
    sm = ex / jnp.sum(ex, axis=0, keepdims=True)
    lb = jnp.sum(sm * lmask_ref[...], axis=0, keepdims=True)
    log_lb = jnp.log(lb)
    log_1mlb = jnp.log1p(-lb)
    one_mlb = 1.0 - lb
    gout = gout_ref[...]

    row = lax.broadcasted_iota(jnp.int32, (c, c), 0)
    col = lax.broadcasted_iota(jnp.int32, (c, c), 1)
    tri = (col <= row).astype(BF16)
    rmod = lax.broadcasted_iota(jnp.int32, (c, w), 0) % HG_SUB
    rmod8 = lax.broadcasted_iota(jnp.int32, (c, w), 0) % SUBLANES_F32
    blk_t = lax.broadcasted_iota(jnp.int32, (HG_HEADS * c, c), 0) % c // HG_SUB
    blk_s = lax.broadcasted_iota(jnp.int32, (HG_HEADS * c, c), 1) // HG_SUB
    same_block = blk_t == blk_s
    lane_head = lax.broadcasted_iota(jnp.int32, (c, w), 1) // HG_DK
    eh_r = lax.broadcasted_iota(jnp.int32, (w, w), 0) // HG_DK
    eh_c = lax.broadcasted_iota(jnp.int32, (w, w), 1) // HG_DK
    same_head = eh_r == eh_c
    eh = same_head.astype(BF16)
    n_sub = c // HG_SUB
    cat_head = lax.broadcasted_iota(jnp.int32, (c, (n_sub - 1) * w), 1) % w // HG_DK

    def chunk(ci, carry):
        r0 = pl.multiple_of(ci * c, c)
        blk = p_ref[pl.ds(r0, c), :].astype(F32)
        hq, hf, hv, hg = blk[:, :w], blk[:, w:2 * w], blk[:, 2 * w:3 * w], blk[:, 3 * w:]
        q = hq * _sigmoid(hq)
        e = jnp.exp(-jnp.abs(hf))
        l1pe = jnp.log(1.0 + e)
        logsig = jnp.minimum(hf, 0.0) - l1pe
        b_term = log_1mlb + logsig
        mxab = jnp.maximum(log_lb, b_term)
        lf = mxab + jnp.log(1.0 + jnp.exp(-jnp.abs(log_lb - b_term)))
        kk = one_mlb * jnp.where(hf >= 0.0, e, 1.0) / (1.0 + e)
        v_bf = hv.astype(BF16)

        lf_hi, lf_lo = _split_bf16(lf)
        bcum = _dot(tri, lf_hi) + _dot(tri, lf_lo)
        b_last = bcum[c - 1:c, :]

        st = st_ref[...]
        qe = (q * jnp.exp(bcum)).astype(BF16)
        o = _dot_nt(qe, st.astype(BF16))

        qms, kms = [], []
        for i in range(1, n_sub):
            lo_r, hi_r = i * HG_SUB, (i + 1) * HG_SUB
            ref_row = bcum[lo_r - 1:lo_r, :]
            qi = q[lo_r:hi_r] * jnp.exp(bcum[lo_r:hi_r] - ref_row)
            pieces = [jnp.zeros((lo_r, w), F32), qi]
            if hi_r < c:
                pieces.append(jnp.zeros((c - hi_r, w), F32))
            qms.append(jnp.concatenate(pieces, axis=0))
            ki = kk[:lo_r] * jnp.exp(ref_row - bcum[:lo_r])
            kms.append(jnp.concatenate([ki, jnp.zeros((c - lo_r, w), F32)], axis=0))
        qc = jnp.concatenate(qms, axis=1)
        kc = jnp.concatenate(kms, axis=1).astype(BF16)
        qstack = jnp.concatenate(
            [jnp.where(cat_head == h, qc, 0.0) for h in range(HG_HEADS)], axis=0).astype(BF16)
        a_stack = _dot_nt(qstack, kc)

        half = HG_SUB // 2
        ref2 = jnp.concatenate(
            [jnp.broadcast_to(bcum[i * HG_SUB + half - 1:i * HG_SUB + half, :], (HG_SUB, w))
             for i in range(n_sub)], axis=0)
        upper = rmod >= half
        q2 = jnp.where(upper, q * jnp.exp(jnp.where(upper, bcum - ref2, 0.0)), 0.0)
        k2 = jnp.where(upper, 0.0, kk * jnp.exp(jnp.where(upper, 0.0, ref2 - bcum))).astype(BF16)
        q2stack = jnp.concatenate(
            [jnp.where(lane_head == h, q2, 0.0) for h in range(HG_HEADS)], axis=0).astype(BF16)
        a2 = _dot_nt(q2stack, k2)
        a_tot = a_stack + jnp.where(same_block, a2, 0.0)
        r_full = _dot(a_tot.astype(BF16), v_bf)
        for h in range(HG_HEADS):
            o = o + jnp.where(lane_head == h, r_full[h * c:(h + 1) * c], 0.0)

        def shift(a, d):
            a3 = a.reshape(c // SUBLANES_F32, SUBLANES_F32, w)
            return pltpu.roll(a3, d, 1).reshape(c, w)

        for d in range(half):
            if d == 0:
                ks, bs, vs = kk, bcum, hv
            else:
                ks, bs, vs = shift(kk, d), shift(bcum, d), shift(hv, d)
            valid = rmod8 >= d
            pd = jnp.where(valid, q * ks * jnp.exp(jnp.where(valid, bcum - bs, 0.0)), 0.0)
            o = o + _dot(pd.astype(BF16), eh) * vs

        kdec = (kk * jnp.exp(b_last - bcum)).astype(BF16)
        upd = _dot_tn(v_bf, kdec)
        st_ref[...] = st * jnp.exp(b_last) + jnp.where(same_head, upd, 0.0)

        o2_hi, o2_lo = _split_bf16(o * o)
        ms = (_dot(o2_hi, eh) + _dot(o2_lo, eh)) * (1.0 / HG_DV)
        y = o * lax.rsqrt(ms + EPS) * gout * (hg * _sigmoid(hg))
        o_ref[pl.ds(r0, c), :] = y.astype(o_ref.dtype)
        return carry

    def trip(ti, carry):
        for u in range(HG_UNROLL):
            carry = chunk(ti * HG_UNROLL + u, carry)
        return carry

    lax.fori_loop(0, n_chunks // HG_UNROLL, trip, 0)


def _hgrn(proj, lb_logits, lmasks, gout_t, l, batch, seq, ts):
    t = proj.shape[0]
    nsb = seq // ts
    cb = COL_HG // (4 * HG_W)
    return pl.pallas_call(
        functools.partial(_hgrn_kernel, n_chunks=ts // HG_CHUNK),
        grid=(batch, nsb),
        in_specs=[pl.BlockSpec((ts, 4 * HG_W), lambda b, s: (b * nsb + s, cb)),
                  pl.BlockSpec(lb_logits.shape, lambda b, s: (0, 0)),
                  _layer_spec(lmasks, l),
                  _layer_spec(gout_t, l)],
        out_specs=pl.BlockSpec((ts, HG_W), lambda b, s: (b * nsb + s, 0)),
        out_shape=jax.ShapeDtypeStruct((t, HG_W), BF16),
        scratch_shapes=[pltpu.VMEM((HG_W, HG_W), F32)],
        compiler_params=_cparams(("parallel", "arbitrary")),
        name="hgrn2",
    )(proj, lb_logits, lmasks, gout_t)


def _mla_prep_kernel(p_ref, gq_ref, wuq_ref, gkv_ref, wkc_ref, wuv_ref, gqq_ref, gqk_ref,
                     cos_ref, sin_ref, q_ref, k_ref, v_ref):
    qk_w = MLA_HEADS * HEAD_PAD
    blk = p_ref[...].astype(F32)
    cq = blk[:, :MLA_Q_RANK]
    ckv = blk[:, MLA_Q_RANK:MLA_Q_RANK + MLA_KV_RANK]
    krp = blk[:, MLA_Q_RANK + MLA_KV_RANK:]

    def rms(x, g):
        ms = jnp.mean(x * x, axis=-1, keepdims=True)
        return x * lax.rsqrt(ms + EPS) * g

    cqn = rms(cq, gq_ref[...]).astype(BF16)
    ckvn = rms(ckv, gkv_ref[...]).astype(BF16)
    q_raw = _dot(cqn, wuq_ref[...])
    k_raw = _dot(jnp.concatenate([ckvn, krp.astype(BF16)], axis=1), wkc_ref[...])
    v_ref[...] = _dot(ckvn, wuv_ref[...]).astype(v_ref.dtype)

    cos, sin = cos_ref[...], sin_ref[...]

    def head(raw, h, g_ref, scale):
        x = raw[:, h * HEAD_PAD:(h + 1) * HEAD_PAD]
        xs = raw[:, qk_w + h * HEAD_PAD:qk_w + (h + 1) * HEAD_PAD]
        ms = jnp.sum(x * x, axis=-1, keepdims=True) * (1.0 / MLA_QK)
        r = lax.rsqrt(ms + EPS) * scale
        return (x * g_ref[:, :HEAD_PAD] * cos + xs * g_ref[:, HEAD_PAD:] * sin) * r

    for h in range(MLA_HEADS):
        sl = slice(h * HEAD_PAD, (h + 1) * HEAD_PAD)
        q_ref[:, sl] = head(q_raw, h, gqq_ref, MLA_QK ** -0.5 * LOG2E).astype(q_ref.dtype)
        k_ref[:, sl] = head(k_raw, h, gqk_ref, 1.0).astype(k_ref.dtype)


def _mla_prep(proj, gq, wuq, gkv, wkc, wuv, gqq, gqk, cos_t, sin_t, l, seq, tm):
    t = proj.shape[0]
    nsb = seq // tm
    cb = COL_MLA // MLA_W
    full = lambda a: _layer_spec(a, l)
    tab = pl.BlockSpec((tm, HEAD_PAD), lambda i: (i % nsb, 0))
    qk_w = MLA_HEADS * HEAD_PAD
    return pl.pallas_call(
        _mla_prep_kernel,
        grid=(t // tm,),
        in_specs=[pl.BlockSpec((tm, MLA_W), lambda i: (i, cb)),
                  full(gq), full(wuq), full(gkv), full(wkc), full(wuv), full(gqq), full(gqk),
                  tab, tab],
        out_specs=[pl.BlockSpec((tm, qk_w), lambda i: (i, 0)),
                   pl.BlockSpec((tm, qk_w), lambda i: (i, 0)),
                   pl.BlockSpec((tm, MLA_HEADS * MLA_V), lambda i: (i, 0))],
        out_shape=[jax.ShapeDtypeStruct((t, qk_w), BF16),
                   jax.ShapeDtypeStruct((t, qk_w), BF16),
                   jax.ShapeDtypeStruct((t, MLA_HEADS * MLA_V), BF16)],
        compiler_params=_cparams(("parallel",)),
        name="mla_prep",
    )(proj, gq, wuq, gkv, wkc, wuv, gqq, gqk, cos_t, sin_t)


def _mla_attn_kernel(q_ref, k_ref, v_ref, o_ref, *, tq, tk):
    i = pl.program_id(2)
    n_diag = tq // tk
    lane = lax.broadcasted_iota(jnp.int32, (tq, LANES), 1)
    qs = [q_ref[:, hh * HEAD_PAD:(hh + 1) * HEAD_PAD] for hh in range(2)]

    row = lax.broadcasted_iota(jnp.int32, (tq, tk), 0)
    col = lax.broadcasted_iota(jnp.int32, (tq, tk), 1)

    def step(kc, carry, dj):
        k0 = pl.multiple_of(kc * tk, tk)
        vb = v_ref[pl.ds(k0, tk), :]
        out = []
        for hh in range(2):
            m, l, acc = carry[hh]
            kb = k_ref[pl.ds(k0, tk), hh * HEAD_PAD:(hh + 1) * HEAD_PAD]
            s = _dot_nt(qs[hh], kb)
            if dj is not None:
                s = jnp.where(col + dj * tk <= row, s, -jnp.inf)
            m_new = jnp.maximum(m, jnp.max(s, axis=-1, keepdims=True))
            alpha = jnp.exp2(m - m_new)
            p = jnp.exp2(s - m_new)
            l = alpha * l + jnp.sum(p, axis=-1, keepdims=True)
            acc = alpha * acc + _dot(p.astype(BF16), vb)
            out.append((m_new, l, acc))
        return tuple(out)

    one = (jnp.full((tq, 1), -jnp.inf, F32), jnp.zeros((tq, 1), F32), jnp.zeros((tq, LANES), F32))
    carry = step(i * n_diag, (one, one), 0)
    for dj in range(1, n_diag):
        carry = step(i * n_diag + dj, carry, dj)
    carry = lax.fori_loop(0, i * n_diag, lambda kc, cr: step(kc, cr, None), carry)
    outs = [acc / l for (_, l, acc) in carry]
    o_ref[...] = jnp.where(lane < MLA_V, outs[0], outs[1]).astype(o_ref.dtype)


def _mla_attn(q, k, v, batch, seq, tq, tk):
    t = q.shape[0]
    nqb = seq // tq
    return pl.pallas_call(
        functools.partial(_mla_attn_kernel, tq=tq, tk=tk),
        grid=(batch, MLA_HEADS // 2, nqb),
        in_specs=[pl.BlockSpec((tq, 2 * HEAD_PAD), lambda b, p, i: (b * nqb + i, p)),
                  pl.BlockSpec((seq, 2 * HEAD_PAD), lambda b, p, i: (b, p)),
                  pl.BlockSpec((seq, LANES), lambda b, p, i: (b, p))],
        out_specs=pl.BlockSpec((tq, LANES), lambda b, p, i: (b * nqb + i, p)),
        out_shape=jax.ShapeDtypeStruct((t, MLA_HEADS * MLA_V), BF16),
        compiler_params=_cparams(("parallel", "parallel", "arbitrary")),
        name="mla_attn",
    )(q, k, v)


CONV_HALO = 32
CONV_ROWS = 64


SUBLANES = SUBLANES_F32


def _conv_kernel(p_ref, w_ref, b_ref, g_ref, beta_ref, o_ref, halo_ref, v_ref, *, ts):
    @pl.when(pl.program_id(1) == 0)
    def _():
        halo_ref[...] = jnp.zeros_like(halo_ref)

    blk = p_ref[...].astype(F32)
    u = blk[:, :CONV_CH] * _sigmoid(blk[:, CONV_CH:])
    u_ext = jnp.concatenate([halo_ref[...], u], axis=0)
    halo_ref[...] = u[ts - CONV_HALO:]
    n_ext = CONV_HALO + ts
    v_ref[0] = u_ext
    for m in range(1, SUBLANES):
        v_ref[m] = pltpu.roll(u_ext, n_ext - m, 0)
    w = w_ref[...]
    bias = b_ref[...]
    gam, beta = g_ref[...], beta_ref[...]
    first = CONV_HALO - (CONV_K - 1)
    for r in range(ts // CONV_ROWS):
        acc = jnp.zeros((CONV_ROWS, CONV_CH), F32) + bias
        for j in range(CONV_K):
            off = first + j
            start = r * CONV_ROWS + (off // SUBLANES) * SUBLANES
            acc = acc + w[j:j + 1, :] * v_ref[off % SUBLANES, start:start + CONV_ROWS, :]
        mu = jnp.mean(acc, axis=-1, keepdims=True)
        xc = acc - mu
        var = jnp.mean(xc * xc, axis=-1, keepdims=True)
        y = xc * lax.rsqrt(var + LN_EPS) * gam + beta
        o_ref[r * CONV_ROWS:(r + 1) * CONV_ROWS, :] = (y * _sigmoid(y)).astype(o_ref.dtype)


def _conformer_conv(proj, w, b, g, beta, l, batch, seq, ts):
    t = proj.shape[0]
    nsb = seq // ts
    cb = COL_CU // (2 * CONV_CH)
    full = lambda a: _layer_spec(a, l)
    return pl.pallas_call(
        functools.partial(_conv_kernel, ts=ts),
        grid=(batch, nsb),
        in_specs=[pl.BlockSpec((ts, 2 * CONV_CH), lambda bb, s: (bb * nsb + s, cb)),
                  full(w), full(b), full(g), full(beta)],
        out_specs=pl.BlockSpec((ts, CONV_CH), lambda bb, s: (bb * nsb + s, 0)),
        out_shape=jax.ShapeDtypeStruct((t, CONV_CH), BF16),
        scratch_shapes=[pltpu.VMEM((CONV_HALO, CONV_CH), F32),
                        pltpu.VMEM((SUBLANES, CONV_HALO + ts, CONV_CH), F32)],
        compiler_params=_cparams(("parallel", "arbitrary")),
        name="conformer_conv",
    )(proj, w, b, g, beta)


SB_SUB = 256


def _sb_attn_kernel(q_ref, k_ref, v_ref, o_ref, *, tq, tk):
    i = pl.program_id(2)
    n_diag = tq // tk
    n_sub = tk // SB_SUB
    lane = lax.broadcasted_iota(jnp.int32, (tq, LANES), 1)
    r2 = lax.broadcasted_iota(jnp.int32, (SB_SUB, SB_SUB), 0)
    c2 = lax.broadcasted_iota(jnp.int32, (SB_SUB, SB_SUB), 1)
    suffix = (r2 >= c2).astype(BF16)
    q = q_ref[...]
    qs = [jnp.where((lane >= hh * SB_HEAD_DIM) & (lane < (hh + 1) * SB_HEAD_DIM), q,
                    jnp.zeros_like(q)) for hh in range(2)]

    def softplus2(z):
        return jnp.maximum(z, 0.0) + jnp.log(1.0 + jnp.exp2(_neg_abs(z))) * LOG2E

    def step(kc, carry):
        k0 = pl.multiple_of(kc * tk, tk)
        kb = k_ref[pl.ds(k0, tk), :]
        vb = v_ref[pl.ds(k0, tk), :]
        out = []
        for hh in range(2):
            run, acc = carry[hh]
            z_all = _dot_nt(qs[hh], kb)
            wgts = [None] * n_sub
            for sub in reversed(range(n_sub)):
                z = z_all[:, sub * SB_SUB:(sub + 1) * SB_SUB]
                incl = _dot(softplus2(z).astype(BF16), suffix)
                wgts[sub] = jnp.exp2(z - incl - run).astype(BF16)
                run = run + incl[:, :1]
            wcat = wgts[0] if n_sub == 1 else jnp.concatenate(wgts, axis=1)
            out.append((run, acc + _dot(wcat, vb)))
        return tuple(out)

    def diag_step(kc, carry, dj):
        k0 = pl.multiple_of(kc * tk, tk)
        out = []
        for hh in range(2):
            run, acc = carry[hh]
            for sub in reversed(range(n_sub)):
                r0 = dj * tk + sub * SB_SUB
                if r0 >= tq:
                    continue
                nr = tq - r0
                kb = k_ref[pl.ds(k0 + sub * SB_SUB, SB_SUB), :]
                vb = v_ref[pl.ds(k0 + sub * SB_SUB, SB_SUB), :]
                z = _dot_nt(qs[hh][r0:], kb)
                valid = (lax.broadcasted_iota(jnp.int32, (nr, SB_SUB), 1)
                         < lax.broadcasted_iota(jnp.int32, (nr, SB_SUB), 0))
                incl = _dot(jnp.where(valid, softplus2(z), 0.0).astype(BF16), suffix)
                wgt = jnp.where(valid, jnp.exp2(z - incl - run[r0:]), 0.0).astype(BF16)
                run_lo = run[r0:] + incl[:, :1]
                acc_lo = acc[r0:] + _dot(wgt, vb)
                if r0 > 0:
                    run = jnp.concatenate([run[:r0], run_lo], axis=0)
                    acc = jnp.concatenate([acc[:r0], acc_lo], axis=0)
                else:
                    run, acc = run_lo, acc_lo
            out.append((run, acc))
        return tuple(out)

    one = (jnp.zeros((tq, 1), F32), jnp.zeros((tq, LANES), F32))
    carry = (one, one)
    for dj in reversed(range(n_diag)):
        carry = diag_step(i * n_diag + dj, carry, dj)
    nfull = i * n_diag
    carry = lax.fori_loop(0, nfull, lambda it, cr: step(nfull - 1 - it, cr), carry)
    o_ref[...] = jnp.where(lane < SB_HEAD_DIM, carry[0][1], carry[1][1]).astype(o_ref.dtype)


def _sb_attn(proj, batch, seq, tq, tk):
    t = proj.shape[0]
    nqb = seq // tq
    qc, kc, vc = (COL_SB // LANES, (COL_SB + 256) // LANES, (COL_SB + 512) // LANES)
    return pl.pallas_call(
        functools.partial(_sb_attn_kernel, tq=tq, tk=tk),
        grid=(batch, SB_HEADS // 2, nqb),
        in_specs=[pl.BlockSpec((tq, LANES), lambda b, p, i: (b * nqb + i, qc + p)),
                  pl.BlockSpec((seq, LANES), lambda b, p, i: (b, kc + p)),
                  pl.BlockSpec((seq, LANES), lambda b, p, i: (b, vc + p))],
        out_specs=pl.BlockSpec((tq, LANES), lambda b, p, i: (b * nqb + i, p)),
        out_shape=jax.ShapeDtypeStruct((t, SB_HEADS * SB_HEAD_DIM), BF16),
        compiler_params=_cparams(("parallel", "parallel", "arbitrary")),
        name="sb_attn",
    )(proj, proj, proj)


def _merge_kernel(x_ref, gl_ref, ya_ref, yb_ref, yc_ref, yd_ref, wb_ref, wo_ref, o_ref):
    merged = None
    for n, y_ref in enumerate((ya_ref, yb_ref, yc_ref, yd_ref)):
        hb = _dot(y_ref[...], wb_ref[n])
        th = jnp.tanh(gl_ref[:, n * D_MODEL:(n + 1) * D_MODEL].astype(F32))
        term = hb + hb * th
        merged = term if merged is None else merged + term
    o_ref[...] = x_ref[...] + _dot(merged.astype(BF16), wo_ref[...])


def _merge(x2d, proj, ya, yb, yc, yd, wb, wo, l, tm):
    t, d = x2d.shape
    yspec = pl.BlockSpec((tm, BRANCH_W), lambda i: (i, 0))
    return pl.pallas_call(
        _merge_kernel,
        grid=(t // tm,),
        in_specs=[pl.BlockSpec((tm, d), lambda i: (i, 0)),
                  pl.BlockSpec((tm, N_BRANCH * d), lambda i: (i, COL_GATE // (N_BRANCH * d))),
                  yspec, yspec, yspec, yspec,
                  _layer_spec(wb, l), _layer_spec(wo, l)],
        out_specs=pl.BlockSpec((tm, d), lambda i: (i, 0)),
        out_shape=jax.ShapeDtypeStruct((t, d), F32),
        compiler_params=_cparams(("parallel",)),
        name="merge_out",
    )(x2d, proj, ya, yb, yc, yd, wb, wo)


FFN_HALO = 16
FFN_ROW_SPLIT = 1


def _ffn_kernel(x_ref, xh_ref, g_ref, wv_ref, wg_ref, cwv_ref, cwg_ref, cbv_ref, cbg_ref,
                wd_ref, o_ref, h_ref, acc_ref, act_a, act_b, *, nk, n_row_blocks, blocks_per_seq):
    s = pl.program_id(0)
    c = s % nk
    i = jnp.minimum(s // nk, n_row_blocks - 1)

    def norm(x):
        ms = jnp.mean(x * x, axis=-1, keepdims=True)
        return (x * lax.rsqrt(ms + EPS) * g_ref[...]).astype(BF16)

    @pl.when(s == 0)
    def _():
        act_a[...] = jnp.zeros_like(act_a)
        acc_ref[...] = jnp.zeros_like(acc_ref)

    @pl.when(c == 0)
    def _():
        h_ref[FFN_HALO:, :] = norm(x_ref[...])
        keep = (i % blocks_per_seq != 0).astype(F32)
        h_ref[0:FFN_HALO, :] = norm(xh_ref[...] * keep)

    def conv(ext, cw_ref, cb_ref):
        cw = cw_ref[...]
        u1 = pltpu.roll(ext, 1, 0)[FFN_HALO:]
        u2 = pltpu.roll(ext, 2, 0)[FFN_HALO:]
        return cw[2:3] * ext[FFN_HALO:] + cw[1:2] * u1 + cw[0:1] * u2 + cb_ref[...]

    tm = acc_ref.shape[0]
    rs = tm // FFN_ROW_SPLIT

    def main(act_in, act_out):
        tails = None
        for j in range(FFN_ROW_SPLIT):
            lo = FFN_HALO + j * rs
            if j == 0:
                hs = h_ref[0:lo + rs, :]
                exts = [_dot(hs, w_ref[...]) for w_ref in (wv_ref, wg_ref)]
            else:
                hs = h_ref[lo:lo + rs, :]
                exts = [jnp.concatenate([t, _dot(hs, w_ref[...])], axis=0)
                        for t, w_ref in zip(tails, (wv_ref, wg_ref))]
            tails = [e[rs:] for e in exts]
            val = conv(exts[0], cwv_ref, cbv_ref)
            hg = conv(exts[1], cwg_ref, cbg_ref)
            act_out[j * rs:(j + 1) * rs, :] = ((hg + hg * jnp.tanh(hg)) * val).astype(BF16)
        acc_ref[...] += _dot(act_in[...], wd_ref[...])

    @pl.when(s % 2 == 0)
    def _():
        main(act_a, act_b)

    @pl.when(s % 2 == 1)
    def _():
        main(act_b, act_a)

    @pl.when(c == 0)
    def _():
        o_ref[...] = acc_ref[...]
        acc_ref[...] = x_ref[...]


def _ffn(x2d, g, w_up, cw, cb, w_down, l, seq, tm, tk):
    t, d = x2d.shape
    nk = D_FF // tk
    hb = tm // FFN_HALO
    nrb = t // tm
    row_blk = lambda s: jnp.minimum(s // nk, nrb - 1)
    return pl.pallas_call(
        functools.partial(_ffn_kernel, nk=nk, n_row_blocks=nrb, blocks_per_seq=seq // tm),
        grid=(nrb * nk + 1,),
        in_specs=[pl.BlockSpec((tm, d), lambda s: (row_blk(s), 0)),
                  pl.BlockSpec((FFN_HALO, d), lambda s: (jnp.maximum(row_blk(s) * hb - 1, 0), 0)),
                  _layer_spec(g, l),
                  pl.BlockSpec((None, d, tk), lambda s: (l, 0, s % nk)),
                  pl.BlockSpec((None, d, tk), lambda s: (l, 0, nk + s % nk)),
                  pl.BlockSpec((None, FFN_CONV_K, tk), lambda s: (l, 0, s % nk)),
                  pl.BlockSpec((None, FFN_CONV_K, tk), lambda s: (l, 0, nk + s % nk)),
                  pl.BlockSpec((None, 1, tk), lambda s: (l, 0, s % nk)),
                  pl.BlockSpec((None, 1, tk), lambda s: (l, 0, nk + s % nk)),
                  pl.BlockSpec((None, tk, d), lambda s: (l, (s + nk - 1) % nk, 0))],
        out_specs=pl.BlockSpec((tm, d), lambda s: (jnp.maximum(s - 1, 0) // nk, 0)),
        out_shape=jax.ShapeDtypeStruct((t, d), F32),
        scratch_shapes=[pltpu.VMEM((tm + FFN_HALO, d), BF16), pltpu.VMEM((tm, d), F32),
                        pltpu.VMEM((tm, tk), BF16), pltpu.VMEM((tm, tk), BF16)],
        compiler_params=_cparams(("arbitrary",)),
        name="ffn",
    )(x2d, x2d, g, w_up, w_up, cw, cw, cb, cb, w_down)


def _prep_w_in(w_in):
    offs = np.concatenate([[0], np.cumsum(IN_COLS)])
    seg = lambda i, c=1.0: (w_in[..., offs[i]:offs[i + 1]] * c).astype(BF16)
    pad = jnp.zeros(w_in.shape[:-1] + (MLA_W - MLA_Q_RANK - MLA_KV_RANK - MLA_ROPE,), BF16)
    parts = [seg(11, 0.5), seg(0), seg(1), seg(2), seg(3), seg(4), seg(5), seg(6), pad,
             seg(7), seg(8, SB_HEAD_DIM ** -0.5 * LOG2E), seg(9), seg(10)]
    return jnp.concatenate(parts, axis=-1)


def _prep_mla_weights(w_uq, w_ukv):
    nl = w_uq.shape[0]
    wq = w_uq.reshape(nl, MLA_Q_RANK, MLA_HEADS, MLA_QK)
    wq = jnp.pad(wq, ((0, 0), (0, 0), (0, 0), (0, HEAD_PAD - MLA_QK)))
    wq = wq.reshape(nl, MLA_Q_RANK, MLA_HEADS * HEAD_PAD)
    wkv = w_ukv.reshape(nl, MLA_KV_RANK, MLA_HEADS, MLA_NOPE + MLA_V)
    wk = jnp.pad(wkv[..., :MLA_NOPE], ((0, 0), (0, 0), (0, 0), (0, HEAD_PAD - MLA_NOPE)))
    wk = wk.reshape(nl, MLA_KV_RANK, MLA_HEADS * HEAD_PAD)
    wv = wkv[..., MLA_NOPE:].reshape(nl, MLA_KV_RANK, MLA_HEADS * MLA_V)
    place = np.zeros((LANES, MLA_HEADS * HEAD_PAD), np.float32)
    for h in range(MLA_HEADS):
        for r in range(MLA_ROPE):
            place[r, h * HEAD_PAD + MLA_NOPE + r] = 1.0
    wkc = jnp.concatenate([wk, jnp.broadcast_to(jnp.asarray(place), (nl,) + place.shape)], axis=1)
    wq = jnp.concatenate([wq, _swap_rope_lanes(wq)], axis=-1)
    wkc = jnp.concatenate([wkc, _swap_rope_lanes(wkc)], axis=-1)
    return wq.astype(BF16), wkc.astype(BF16), wv.astype(BF16)


def _swap_rope_lanes(a):
    half = MLA_ROPE // 2
    perm = np.arange(HEAD_PAD)
    perm[MLA_NOPE:MLA_NOPE + half] = np.arange(MLA_NOPE + half, MLA_QK)
    perm[MLA_NOPE + half:MLA_QK] = np.arange(MLA_NOPE, MLA_NOPE + half)
    groups = a.shape[-1] // HEAD_PAD
    idx = (np.arange(groups)[:, None] * HEAD_PAD + perm[None, :]).reshape(-1)
    return a[..., idx]


def _qk_gain(g):
    gp = jnp.pad(g, ((0, 0), (0, HEAD_PAD - MLA_QK)))
    return jnp.concatenate([gp, _swap_rope_lanes(gp)], axis=-1)[:, None, :]


def _rope_tables(seq):
    half = MLA_ROPE // 2
    freqs = ROPE_BASE ** (-jnp.arange(half, dtype=F32) / half)
    ang = jnp.arange(seq, dtype=F32)[:, None] * freqs[None, :]
    cos, sin = jnp.cos(ang), jnp.sin(ang)
    tail = HEAD_PAD - MLA_QK
    cos_t = jnp.concatenate([jnp.ones((seq, MLA_NOPE), F32), cos, cos, jnp.ones((seq, tail), F32)],
                            axis=1)
    sin_t = jnp.concatenate([jnp.zeros((seq, MLA_NOPE), F32), -sin, sin, jnp.zeros((seq, tail), F32)],
                            axis=1)
    return cos_t, sin_t


def kernel(x, g_mix, w_in, lb_logits, g_hg_out, g_q_lat, w_uq, g_kv_lat, w_ukv, g_qk_q, g_qk_k,
           conv_w, conv_b, conv_ln_g, conv_ln_b, w_branch, w_out, g_ffn, w_up, ffn_conv_w,
           ffn_conv_b, w_down):
    batch, seq, d = x.shape
    depth = w_in.shape[0]
    t = batch * seq

    tm_proj = min(1024, seq)
    ts_hg = min(1024, seq)
    tm_mla = min(1024, seq)
    tq_attn = min(1024, seq)
    tk_attn = min(1024, seq)
    ts_conv = min(1024, seq)
    tm_merge = min(1024, seq)
    tm_ffn = min(1024, seq)

    w_in_p = _prep_w_in(w_in)
    wq_p, wkc_p, wv_p = _prep_mla_weights(w_uq, w_ukv)
    cos_t, sin_t = _rope_tables(seq)
    rows = lambda a: a[:, None, :]
    gqq_p = _qk_gain(g_qk_q)
    gqk_p = _qk_gain(g_qk_k)
    gout_t = rows(jnp.tile(g_hg_out, (1, HG_HEADS)))
    w_branch_b = (0.5 * w_branch).astype(BF16)
    w_out_b = w_out.astype(BF16)
    w_up_b = w_up.astype(BF16)
    w_down_b = w_down.astype(BF16)
    ids = np.arange(depth)
    lmasks = jnp.asarray(((ids[None, :] >= 1) & (ids[None, :] <= ids[:, None]))
                         .astype(np.float32)[:, :, None])
    lb_logits = lb_logits.astype(F32)
    g_mix3, g_ffn3, g_q3, g_kv3 = rows(g_mix), rows(g_ffn), rows(g_q_lat), rows(g_kv_lat)
    conv_b3, ln_g3, ln_b3 = rows(conv_b), rows(conv_ln_g), rows(conv_ln_b)
    gate_half = jnp.asarray(np.concatenate([np.ones(D_FF, np.float32), np.full(D_FF, 0.5, np.float32)]))
    ffn_cw = ffn_conv_w * gate_half
    ffn_b3 = rows(ffn_conv_b * gate_half)

    x2d = x.reshape(t, d)
    for l in range(depth):
        proj = _norm_matmul(x2d, g_mix3, w_in_p, l, tm_proj, TN_PROJ)
        y_a = _hgrn(proj, lb_logits, lmasks, gout_t, l, batch, seq, ts_hg)
        q, k, v = _mla_prep(proj, g_q3, wq_p, g_kv3, wkc_p, wv_p, gqq_p, gqk_p,
                            cos_t, sin_t, l, seq, tm_mla)
        y_b = _mla_attn(q, k, v, batch, seq, tq_attn, tk_attn)
        y_c = _conformer_conv(proj, conv_w, conv_b3, ln_g3, ln_b3, l, batch, seq, ts_conv)
        y_d = _sb_attn(proj, batch, seq, tq_attn, tk_attn)
        x2d = _merge(x2d, proj, y_a, y_b, y_c, y_d, w_branch_b, w_out_b, l, tm_merge)
        x2d = _ffn(x2d, g_ffn3, w_up_b, ffn_cw, ffn_b3, w_down_b, l, seq, tm_ffn, TK_FFN)
    return x2d.reshape(batch, seq, d)
```
